```python
import math
import jax, jax.numpy as jnp
from jax import lax
import numpy as np

D_MODEL = 2048
BATCH = 4
SEQ = 2048
DEPTH = 4
DEC_BATCH = 8
DEC_SEQ = 4
PAST_LEN = 16384
PAGE_SIZE = 128

HEAD_DIM = 128
MIX_WIDTH = D_MODEL
WIDTH_A = MIX_WIDTH // 2
WIDTH_B = MIX_WIDTH - WIDTH_A
H_A = WIDTH_A // HEAD_DIM
DQK_A = HEAD_DIM // 2
DV_A = HEAD_DIM
H_B = WIDTH_B // HEAD_DIM
D_B = HEAD_DIM
D_IN = 3 * WIDTH_A + 3 * WIDTH_B
D_FF = 4 * D_MODEL
ROPE_THETA = 500000.0
ROT_A = DQK_A // 4
ROT_B = D_B // 4
MOBA_BLOCK = 256
MOBA_TOPK = 3
DENSE_Q_BLOCK = 128
MOBA_Q_BLOCK = 16
NORM_EPS = 1e-5
POOL_NUM = 5
POOL_DEN = 4

kernel_name = "hybrid_diffattn_moba_decoder_step"


def rmsnorm(x, g):
    xf = x.astype(jnp.float32)
    y = xf * lax.rsqrt(jnp.mean(xf * xf, axis=-1, keepdims=True) + NORM_EPS)
    return (y * g.astype(jnp.float32)).astype(x.dtype)


def rope_partial(x, pos, rot):
    half = rot // 2
    freqs = jnp.exp(-math.log(ROPE_THETA) * jnp.arange(half, dtype=jnp.float32) / half)
    ang = pos.astype(jnp.float32)[:, None] * freqs[None, :]
    cos = jnp.cos(ang)[None, :, None, :]
    sin = jnp.sin(ang)[None, :, None, :]
    xf = x.astype(jnp.float32)
    x1 = xf[..., :half]
    x2 = xf[..., half:rot]
    out = jnp.concatenate([x1 * cos - x2 * sin, x1 * sin + x2 * cos, xf[..., rot:]], axis=-1)
    return out.astype(x.dtype)


def sweep_queries(fn, qs, qpos, block):
    T = qpos.shape[0]
    if T <= block or T % block != 0:
        return fn(qs, qpos)
    n = T // block
    qs_c = tuple(jnp.moveaxis(q.reshape(q.shape[0], n, block, *q.shape[2:]), 1, 0) for q in qs)
    pos_c = qpos.reshape(n, block)
    out = lax.map(lambda a: fn(a[0], a[1]), (qs_c, pos_c))
    out = jnp.moveaxis(out, 0, 1)
    return out.reshape(out.shape[0], T, *out.shape[3:])


def diff_attention(q1, q2, qpos, k1, k2, v, lam):
    kpos = jnp.arange(k1.shape[1])
    scale = DQK_A ** -0.5

    def block(qs, qp):
        a, b = qs
        mask = kpos[None, :] <= qp[:, None]

        def smax(q, k):
            s = jnp.einsum("bqhd,bkhd->bhqk", q, k).astype(jnp.float32) * scale
            return jax.nn.softmax(jnp.where(mask, s, -jnp.inf), axis=-1)

        p = smax(a, k1) - lam * smax(b, k2)
        return jnp.einsum("bhqk,bkhd->bqhd", p.astype(v.dtype), v)

    return sweep_queries(block, (q1, q2), qpos, DENSE_Q_BLOCK)


def moba_attention(q, qpos, k, v):
    B, L, H, D = k.shape
    nb = -(-L // MOBA_BLOCK)
    pad = nb * MOBA_BLOCK - L

    def to_blocks(t):
        t = jnp.pad(t, ((0, 0), (0, pad), (0, 0), (0, 0)))
        return t.reshape(B, nb, MOBA_BLOCK, H, D).transpose(0, 3, 1, 2, 4)

    kb = to_blocks(k)
    vb = to_blocks(v)
    kmean = jnp.mean(kb.astype(jnp.float32), axis=3)
    k_sel = min(MOBA_TOPK, nb)
    scale = D ** -0.5
    blk_off = jnp.arange(MOBA_BLOCK)
    gather = jax.vmap(jax.vmap(lambda t, ix: t[ix]))

    def block(qs, qp):
        (qc,) = qs
        tq = qp.shape[0]
        own = qp // MOBA_BLOCK
        gate = jnp.einsum("bqhd,bhnd->bhqn", qc.astype(jnp.float32), kmean)
        past = jnp.arange(nb)[None, :] < own[:, None]
        gate = jnp.where(past, gate, -jnp.inf)
        _, top = lax.top_k(gate, k_sel)
        own_b = jnp.broadcast_to(own[None, None, :, None], top.shape[:3] + (1,)).astype(top.dtype)
        sel = jnp.concatenate([top, own_b], axis=-1)
        slot_ok = jnp.concatenate(
            [jnp.arange(k_sel)[None, :] < own[:, None], jnp.ones((tq, 1), dtype=bool)], axis=-1)
        kg = gather(kb, sel)
        vg = gather(vb, sel)
        kpos = sel[..., None] * MOBA_BLOCK + blk_off
        mask = slot_ok[None, None, :, :, None] & (kpos <= qp[None, None, :, None, None])
        s = jnp.einsum("bqhd,bhqskd->bhqsk", qc, kg).astype(jnp.float32) * scale
        s = jnp.where(mask, s, -jnp.inf)
        p = jax.nn.softmax(s.reshape(B, H, tq, -1), axis=-1).reshape(s.shape)
        return jnp.einsum("bhqsk,bhqskd->bqhd", p.astype(vg.dtype), vg)

    return sweep_queries(block, (q,), qpos, MOBA_Q_BLOCK)


def layer(x, pos, past, l, p):
    B, T, _ = x.shape
    lam_init = 0.8 - 0.6 * math.exp(-0.3 * l)
    h = rmsnorm(x, p["norm_attn"][l])
    proj = h @ p["w_in"][l]
    qa, ka, va, qb, kbp, vbp = jnp.split(
        proj, [WIDTH_A, 2 * WIDTH_A, 3 * WIDTH_A, 3 * WIDTH_A + WIDTH_B, 3 * WIDTH_A + 2 * WIDTH_B], axis=-1)
    qa = qa.reshape(B, T, H_A, 2 * DQK_A)
    ka = ka.reshape(B, T, H_A, 2 * DQK_A)
    q1 = rope_partial(qa[..., :DQK_A], pos, ROT_A)
    q2 = rope_partial(qa[..., DQK_A:], pos, ROT_A)
    ka_rows = jnp.concatenate(
        [rope_partial(ka[..., :DQK_A], pos, ROT_A), rope_partial(ka[..., DQK_A:], pos, ROT_A)], axis=-1)
    va_rows = va.reshape(B, T, H_A, DV_A)
    qb = rope_partial(qb.reshape(B, T, H_B, D_B), pos, ROT_B)
    kb_rows = rope_partial(kbp.reshape(B, T, H_B, D_B), pos, ROT_B)
    vb_rows = vbp.reshape(B, T, H_B, D_B)
    rows = (ka_rows, va_rows, kb_rows, vb_rows)
    if past is None:
        ka_all, va_all, kb_all, vb_all = rows
    else:
        ka_all, va_all, kb_all, vb_all = (jnp.concatenate([pc, r], axis=1) for pc, r in zip(past, rows))

    lam = (jnp.exp(jnp.sum(p["lambda_q1"][l].astype(jnp.float32) * p["lambda_k1"][l].astype(jnp.float32)))
           - jnp.exp(jnp.sum(p["lambda_q2"][l].astype(jnp.float32) * p["lambda_k2"][l].astype(jnp.float32)))
           + lam_init)
    oa = diff_attention(q1, q2, pos, ka_all[..., :DQK_A], ka_all[..., DQK_A:], va_all, lam)
    oa = rmsnorm(oa, p["subln_a"][l]) * (1.0 - lam_init)
    ob = moba_attention(qb, pos, kb_all, vb_all)
    o = jnp.concatenate([oa.reshape(B, T, WIDTH_A), ob.reshape(B, T, WIDTH_B)], axis=-1) @ p["w_out"][l]
    x = x + o
    h = rmsnorm(x, p["norm_mlp"][l])
    x = x + jnp.square(jax.nn.relu(h @ p["w_up"][l])) @ p["w_down"][l]
    return x, rows


def gather_pages(cache, l, page_table):
    pages = cache[l, page_table]
    b, n, ps = pages.shape[:3]
    return pages.reshape(b, n * ps, *pages.shape[3:])


def setup_inputs(seed: int = 0) -> dict:
    key = jax.random.key(seed)
    ks = jax.random.split(key, 20)
    n_pages = PAST_LEN // PAGE_SIZE
    n_pool = (DEC_BATCH * n_pages * POOL_NUM) // POOL_DEN
    f32 = jnp.float32
    nrm = lambda k, shape, s: jax.random.normal(k, shape, dtype=f32) * s
    page_table = jax.random.permutation(ks[6], n_pool)[: DEC_BATCH * n_pages]
    page_table = page_table.reshape(DEC_BATCH, n_pages).astype(jnp.int32)
    return {
        "x_prompt": nrm(ks[0], (BATCH, SEQ, D_MODEL), 1.0),
        "x_sample": nrm(ks[1], (DEC_BATCH, DEC_SEQ, D_MODEL), 1.0),
        "cache_a_k": nrm(ks[2], (DEPTH, n_pool, PAGE_SIZE, H_A, 2 * DQK_A), 1.0),
        "cache_a_v": nrm(ks[3], (DEPTH, n_pool, PAGE_SIZE, H_A, DV_A), 1.0),
        "cache_b_k": nrm(ks[4], (DEPTH, n_pool, PAGE_SIZE, H_B, D_B), 1.0),
        "cache_b_v": nrm(ks[5], (DEPTH, n_pool, PAGE_SIZE, H_B, D_B), 1.0),
        "page_table": page_table,
        "norm_attn": 1.0 + nrm(ks[7], (DEPTH, D_MODEL), 0.02),
        "w_in": nrm(ks[8], (DEPTH, D_MODEL, D_IN), D_MODEL ** -0.5),
        "lambda_q1": nrm(ks[9], (DEPTH, DQK_A), 0.1),
        "lambda_k1": nrm(ks[10], (DEPTH, DQK_A), 0.1),
        "lambda_q2": nrm(ks[11], (DEPTH, DQK_A), 0.1),
        "lambda_k2": nrm(ks[12], (DEPTH, DQK_A), 0.1),
        "subln_a": 1.0 + nrm(ks[13], (DEPTH, DV_A), 0.02),
        "w_out": nrm(ks[14], (DEPTH, MIX_WIDTH, D_MODEL), MIX_WIDTH ** -0.5),
        "norm_mlp": 1.0 + nrm(ks[15], (DEPTH, D_MODEL), 0.02),
        "w_up": nrm(ks[16], (DEPTH, D_MODEL, D_FF), D_MODEL ** -0.5),
        "w_down": nrm(ks[17], (DEPTH, D_FF, D_MODEL), D_FF ** -0.5),
        "norm_final": 1.0 + nrm(ks[18], (D_MODEL,), 0.02),
    }


def reference(x_prompt, x_sample, cache_a_k, cache_a_v, cache_b_k, cache_b_v, page_table,
              norm_attn, w_in, lambda_q1, lambda_k1, lambda_q2, lambda_k2, subln_a, w_out,
              norm_mlp, w_up, w_down, norm_final):
    params = dict(norm_attn=norm_attn, w_in=w_in, lambda_q1=lambda_q1, lambda_k1=lambda_k1,
                  lambda_q2=lambda_q2, lambda_k2=lambda_k2, subln_a=subln_a, w_out=w_out,
                  norm_mlp=norm_mlp, w_up=w_up, w_down=w_down)
    caches = (cache_a_k, cache_a_v, cache_b_k, cache_b_v)

    def run(x, pos, with_cache):
        all_rows = []
        for l in range(DEPTH):
            past = tuple(gather_pages(c, l, page_table) for c in caches) if with_cache else None
            x, rows = layer(x, pos, past, l, params)
            all_rows.append(rows)
        y = rmsnorm(x, norm_final)
        new = tuple(jnp.stack([r[i] for r in all_rows], axis=0) for i in range(4))
        return y, new

    past_len = page_table.shape[1] * PAGE_SIZE
    y_prompt, (pak, pav, pbk, pbv) = run(x_prompt, jnp.arange(x_prompt.shape[1]), False)
    y_sample, (sak, sav, sbk, sbv) = run(x_sample, past_len + jnp.arange(x_sample.shape[1]), True)
    return (y_prompt, y_sample, pak, pav, pbk, pbv, sak, sav, sbk, sbv)
```

```python
import functools
import math

import jax
import jax.numpy as jnp
from jax import lax
from jax.experimental import pallas as pl
from jax.experimental.pallas import tpu as pltpu

F32 = jnp.float32
BF16 = jnp.bfloat16

HEAD_DIM = 128
DQK_A = HEAD_DIM // 2
ROT_A = DQK_A // 4
ROT_B = HEAD_DIM // 4
ROPE_THETA = 500000.0
MOBA_BLOCK = 256
MOBA_TOPK = 3
NORM_EPS = 1e-5
NEG = -1e30
N_SECTIONS = 6
ROW_PAD = 8
V7X_VMEM_BYTES = 64 * 1024 * 1024

_NT = (((1,), (1,)), ((), ()))


def _vmem_limit(block_bytes):
    want = 2 * block_bytes + 8 * 1024 * 1024
    return int(min(want, V7X_VMEM_BYTES - 8 * 1024 * 1024))


def _nbytes(shape, dtype):
    return math.prod(shape) * jnp.dtype(dtype).itemsize


def _rmsnorm(x, g):
    return x * lax.rsqrt(jnp.mean(x * x, axis=-1, keepdims=True) + NORM_EPS) * g


def _rope_tables(pos, rot, comp_width):
    half = rot // 2
    freqs = jnp.exp(-math.log(ROPE_THETA) * jnp.arange(half, dtype=F32) / half)
    ang = pos.astype(F32)[:, None] * freqs[None, :]
    cos, sin = jnp.cos(ang), jnp.sin(ang)
    n = pos.shape[0]
    rest = comp_width - rot
    zeros_h = jnp.zeros((n, half), F32)
    c = jnp.concatenate([cos, cos, jnp.ones((n, rest), F32)], axis=-1)
    s_next = jnp.concatenate([-sin, zeros_h, jnp.zeros((n, rest), F32)], axis=-1)
    s_prev = jnp.concatenate([zeros_h, sin, jnp.zeros((n, rest), F32)], axis=-1)
    reps = HEAD_DIM // comp_width
    return tuple(jnp.tile(t, (1, reps)) for t in (c, s_next, s_prev))


def _norm_kernel(x_ref, g_ref, o_ref):
    o_ref[...] = _rmsnorm(x_ref[...], g_ref[...]).astype(o_ref.dtype)


def _norm(x, g, *, tm):
    n, d = x.shape
    row = pl.BlockSpec((tm, d), lambda i: (i, 0))
    return pl.pallas_call(
        _norm_kernel,
        grid=(n // tm,),
        in_specs=[row, pl.BlockSpec((1, d), lambda i: (0, 0))],
        out_specs=row,
        out_shape=jax.ShapeDtypeStruct((n, d), BF16),
        compiler_params=pltpu.CompilerParams(
            dimension_semantics=("arbitrary",),
            vmem_limit_bytes=_vmem_limit(_nbytes((tm, d), F32) + _nbytes((tm, d), BF16))),
        name="norm",
    )(x, g)


def _store_heads(y, tabs, half, scale, wide_ref, tall_ref):
    tm = y.shape[0]
    n_heads = y.shape[1] // HEAD_DIM
    if tabs is not None:
        c, s_next, s_prev = (t[...] for t in tabs)
    for h in range(n_heads):
        sl = slice(h * HEAD_DIM, (h + 1) * HEAD_DIM)
        r = y[:, sl]
        if tabs is not None:
            r = (r * c + pltpu.roll(r, HEAD_DIM - half, 1) * s_next + pltpu.roll(r, half, 1) * s_prev)
        if tall_ref is not None:
            tall_ref[pl.ds(h, tm, stride=n_heads), :] = r
        if wide_ref is not None:
            wide_ref[:, sl] = (r * scale if scale != 1.0 else r).astype(wide_ref.dtype)


def _proj_kernel(xn_ref, w_ref, ca, sna, spa, cb, snb, spb,
                 qa_ref, ka_w, ka_t, va_w, va_t, qb_ref, kb_w, kb_t, vb_w, vb_t):
    j = pl.program_id(0)
    y = jnp.dot(xn_ref[...], w_ref[...], preferred_element_type=F32)
    tabs_a, tabs_b = (ca, sna, spa), (cb, snb, spb)
    sections = (
        (tabs_a, ROT_A // 2, DQK_A ** -0.5, qa_ref, None),
        (tabs_a, ROT_A // 2, 1.0, ka_w, ka_t),
        (None, 0, 1.0, va_w, va_t),
        (tabs_b, ROT_B // 2, 1.0, qb_ref, None),
        (tabs_b, ROT_B // 2, 1.0, kb_w, kb_t),
        (None, 0, 1.0, vb_w, vb_t),
    )
    for s, args in enumerate(sections):
        pl.when(j == s)(functools.partial(_store_heads, y, *args))


def _proj(xn, w, tabs_a, tabs_b, *, tm):
    n, d = xn.shape
    sec = w.shape[1] // N_SECTIONS
    n_heads = sec // HEAD_DIM
    n_i = n // tm
    n_tab = tabs_a[0].shape[0] // tm
    tab_spec = pl.BlockSpec((tm, HEAD_DIM), lambda j, i: (i % n_tab, 0))

    def out_spec(section, shape):
        def index_map(j, i):
            return (jnp.where(j < section, 0, jnp.where(j > section, n_i - 1, i)), 0)
        return pl.BlockSpec(shape, index_map)

    wide, tall = (tm, sec), (tm * n_heads, HEAD_DIM)
    wide_sds = lambda dt: jax.ShapeDtypeStruct((n, sec), dt)
    tall_sds = jax.ShapeDtypeStruct((n * n_heads, HEAD_DIM), F32)
    out_specs = [out_spec(0, wide)]
    out_shape = [wide_sds(BF16)]
    for section in (1, 2):
        out_specs += [out_spec(section, wide), out_spec(section, tall)]
        out_shape += [wide_sds(BF16), tall_sds]
    out_specs.append(out_spec(3, wide))
    out_shape.append(wide_sds(F32))
    for section in (4, 5):
        out_specs += [out_spec(section, wide), out_spec(section, tall)]
        out_shape += [wide_sds(BF16), tall_sds]
    block_bytes = (_nbytes((tm, d), BF16) + _nbytes((d, sec), BF16) + 6 * _nbytes((tm, HEAD_DIM), F32)
                   + 5 * _nbytes(wide, BF16) + 5 * _nbytes(wide, F32))
    return pl.pallas_call(
        _proj_kernel,
        grid=(N_SECTIONS, n_i),
        in_specs=[pl.BlockSpec((tm, d), lambda j, i: (i, 0)),
                  pl.BlockSpec((d, sec), lambda j, i: (0, j))] + [tab_spec] * 6,
        out_specs=out_specs,
        out_shape=out_shape,
        compiler_params=pltpu.CompilerParams(
            dimension_semantics=("arbitrary", "arbitrary"),
            vmem_limit_bytes=_vmem_limit(block_bytes)),
        name="proj",
    )(xn, w, *tabs_a, *tabs_b)


def _lambda(lq1, lk1, lq2, lk2, lam_init):
    a = jnp.exp(jnp.sum(lq1[...] * lk1[...], axis=-1, keepdims=True))
    b = jnp.exp(jnp.sum(lq2[...] * lk2[...], axis=-1, keepdims=True))
    return a - b + lam_init


def _split_components(q):
    lane = lax.broadcasted_iota(jnp.int32, q.shape, 1)
    zero = jnp.zeros_like(q)
    return jnp.where(lane < DQK_A, q, zero), jnp.where(lane >= DQK_A, q, zero)


def _softmax_first(s, pv):
    m = jnp.max(s, axis=-1, keepdims=True)
    p = jnp.exp(s - m)
    return m, jnp.sum(p, axis=-1, keepdims=True), pv(p)


def _softmax_update(state, s, pv):
    m, l, acc = state
    m_new = jnp.maximum(m, jnp.max(s, axis=-1, keepdims=True))
    alpha = jnp.exp(m - m_new)
    p = jnp.exp(s - m_new)
    return m_new, alpha * l + jnp.sum(p, axis=-1, keepdims=True), alpha * acc + pv(p)


def _pv_bf16(v):
    return lambda p: jnp.dot(p.astype(BF16), v, preferred_element_type=F32)


def _subln(o, g, lam_init):
    return _rmsnorm(o, g) * (1.0 - lam_init)


def _diff_attn_kernel(lq1, lk1, lq2, lk2, g_ref, q_ref, k_ref, v_ref, o_ref, *, tq, lam_init):
    qi = pl.program_id(2)
    q1, q2 = _split_components(q_ref[...])

    def kv(j):
        start = pl.multiple_of(j * tq, tq)
        return k_ref[pl.ds(start, tq), :], v_ref[pl.ds(start, tq), :]

    def scores(q, k):
        return lax.dot_general(q, k, _NT, preferred_element_type=F32)

    k, v = kv(qi)
    row = lax.broadcasted_iota(jnp.int32, (tq, tq), 0)
    col = lax.broadcasted_iota(jnp.int32, (tq, tq), 1)
    causal = col <= row
    st1 = _softmax_first(jnp.where(causal, scores(q1, k), NEG), _pv_bf16(v))
    st2 = _softmax_first(jnp.where(causal, scores(q2, k), NEG), _pv_bf16(v))

    def body(j, carry):
        st1, st2 = carry
        k, v = kv(j)
        return (_softmax_update(st1, scores(q1, k), _pv_bf16(v)),
                _softmax_update(st2, scores(q2, k), _pv_bf16(v)))

    (_, l1, a1), (_, l2, a2) = lax.fori_loop(0, qi, body, (st1, st2))
    lam = _lambda(lq1, lk1, lq2, lk2, lam_init)
    o = a1 / l1 - lam * (a2 / l2)
    o_ref[...] = _subln(o, g_ref[...], lam_init).astype(o_ref.dtype)


def _diff_attn(q, k, v, lams, g, *, tq, lam_init):
    b, t, w = q.shape
    small = pl.BlockSpec((1, DQK_A), lambda bi, h, qi: (0, 0))
    kv_spec = pl.BlockSpec((None, t, HEAD_DIM), lambda bi, h, qi: (bi, 0, h))
    q_spec = pl.BlockSpec((None, tq, HEAD_DIM), lambda bi, h, qi: (bi, qi, h))
    block_bytes = 2 * _nbytes((t, HEAD_DIM), BF16) + 2 * _nbytes((tq, HEAD_DIM), BF16)
    return pl.pallas_call(
        functools.partial(_diff_attn_kernel, tq=tq, lam_init=lam_init),
        grid=(b, w // HEAD_DIM, t // tq),
        in_specs=[small] * 4 + [pl.BlockSpec((1, HEAD_DIM), lambda bi, h, qi: (0, 0)), q_spec, kv_spec, kv_spec],
        out_specs=q_spec,
        out_shape=jax.ShapeDtypeStruct((b, t, w), BF16),
        compiler_params=pltpu.CompilerParams(
            dimension_semantics=("arbitrary",) * 3, vmem_limit_bytes=_vmem_limit(block_bytes)),
        name="diff_attn",
    )(*lams, g, q, k, v)


def _topk_past_mask(gate_t, n_past, nb):
    blk = lax.broadcasted_iota(jnp.int32, gate_t.shape, 0)
    past = blk < n_past
    gate_t = jnp.where(past, gate_t, -jnp.inf)
    rank = jnp.zeros(gate_t.shape, F32)
    for m in range(nb):
        gm = gate_t[m:m + 1, :]
        ahead = (gm > gate_t) | ((gm == gate_t) & (m < blk))
        rank = rank + jnp.where(ahead, 1.0, 0.0)
    return jnp.where(past & (rank < MOBA_TOPK), 1.0, 0.0)


def _moba_kernel(q_ref, k_ref, v_ref, o_ref, kmean_ref, *, nb):
    tq = MOBA_BLOCK
    qi = pl.program_id(2)

    @pl.when(qi == 0)
    def _():
        k = k_ref[...].astype(F32)
        kmean_ref[...] = jnp.zeros(kmean_ref.shape, F32)
        kmean_ref[:nb, :] = jnp.mean(k.reshape(nb, MOBA_BLOCK, HEAD_DIM), axis=1)

    q = q_ref[...]
    gate_t = lax.dot_general(kmean_ref[...], q, _NT, precision=lax.Precision.HIGHEST,
                             preferred_element_type=F32)
    sel_t = _topk_past_mask(gate_t, qi, nb).astype(BF16)
    sel_t = jnp.concatenate([sel_t, jnp.zeros((HEAD_DIM - sel_t.shape[0], tq), BF16)], axis=0)
    eye = (lax.broadcasted_iota(jnp.int32, (tq, tq), 0)
           == lax.broadcasted_iota(jnp.int32, (tq, tq), 1)).astype(BF16)
    sel = lax.dot_general(eye, sel_t, _NT, preferred_element_type=F32)
    sel_lane = lax.broadcasted_iota(jnp.int32, sel.shape, 1)

    qs = (q * HEAD_DIM ** -0.5).astype(BF16)

    def kv(j):
        start = pl.multiple_of(j * tq, tq)
        return k_ref[pl.ds(start, tq), :], v_ref[pl.ds(start, tq), :]

    def scores(k):
        return lax.dot_general(qs, k, _NT, preferred_element_type=F32)

    k, v = kv(qi)
    row = lax.broadcasted_iota(jnp.int32, (tq, tq), 0)
    col = lax.broadcasted_iota(jnp.int32, (tq, tq), 1)
    state = _softmax_first(jnp.where(col <= row, scores(k), NEG), _pv_bf16(v))

    def body(j, state):
        k, v = kv(j)
        chosen = jnp.sum(jnp.where(sel_lane == j, sel, 0.0), axis=-1, keepdims=True)
        return _softmax_update(state, scores(k) + (1.0 - chosen) * NEG, _pv_bf16(v))

    _, l, acc = lax.fori_loop(0, qi, body, state)
    o_ref[...] = (acc / l).astype(o_ref.dtype)


def _moba_attn(q, k, v):
    b, t, w = q.shape
    nb = t // MOBA_BLOCK
    nb_pad = -(-nb // 8) * 8
    kv_spec = pl.BlockSpec((None, t, HEAD_DIM), lambda bi, h, qi: (bi, 0, h))
    q_spec = pl.BlockSpec((None, MOBA_BLOCK, HEAD_DIM), lambda bi, h, qi: (bi, qi, h))
    block_bytes = 2 * _nbytes((t, HEAD_DIM), BF16) + 2 * _nbytes((MOBA_BLOCK, HEAD_DIM), F32)
    return pl.pallas_call(
        functools.partial(_moba_kernel, nb=nb),
        grid=(b, w // HEAD_DIM, nb),
        in_specs=[q_spec, kv_spec, kv_spec],
        out_specs=q_spec,
        out_shape=jax.ShapeDtypeStruct((b, t, w), BF16),
        scratch_shapes=[pltpu.VMEM((nb_pad, HEAD_DIM), F32)],
        compiler_params=pltpu.CompilerParams(
            dimension_semantics=("arbitrary",) * 3, vmem_limit_bytes=_vmem_limit(block_bytes)),
        name="moba_attn",
    )(q, k, v)


def _out_proj_kernel(x_ref, oa_ref, ob_ref, w_ref, o_ref):
    wa = oa_ref.shape[1]
    o_ref[...] = (x_ref[...]
                  + jnp.dot(oa_ref[...], w_ref[:wa, :], preferred_element_type=F32)
                  + jnp.dot(ob_ref[...], w_ref[wa:, :], preferred_element_type=F32))


def _out_proj(x, oa, ob, w, *, tm):
    n, d = x.shape
    wa, wb = oa.shape[1], ob.shape[1]
    row = lambda width: pl.BlockSpec((tm, width), lambda i: (i, 0))
    block_bytes = (2 * _nbytes((tm, d), F32) + _nbytes((tm, wa + wb), BF16) + _nbytes(w.shape, BF16))
    return pl.pallas_call(
        _out_proj_kernel,
        grid=(n // tm,),
        in_specs=[row(d), row(wa), row(wb), pl.BlockSpec(w.shape, lambda i: (0, 0))],
        out_specs=row(d),
        out_shape=jax.ShapeDtypeStruct((n, d), F32),
        compiler_params=pltpu.CompilerParams(
            dimension_semantics=("arbitrary",), vmem_limit_bytes=_vmem_limit(block_bytes)),
        name="out_proj",
    )(x, oa, ob, w)


def _mlp_kernel(x_ref, g_ref, wu_ref, wd_ref, gn_ref, o_ref, *scratch_or_out, last_layer):
    if last_layer:
        (xn_ref,) = scratch_or_out
    else:
        xn_out_ref, xn_ref = scratch_or_out
    f = pl.program_id(1)

    @pl.when(f == 0)
    def _():
        x = x_ref[...]
        xn_ref[...] = _rmsnorm(x, g_ref[...]).astype(BF16)
        o_ref[...] = x

    h = jnp.dot(xn_ref[...], wu_ref[...], preferred_element_type=F32)
    h = jnp.square(jnp.maximum(h, 0.0)).astype(BF16)
    o_ref[...] += jnp.dot(h, wd_ref[...], preferred_element_type=F32)

    @pl.when(f == pl.num_programs(1) - 1)
    def _():
        y = _rmsnorm(o_ref[...], gn_ref[...])
        if last_layer:
            o_ref[...] = y
        else:
            xn_out_ref[...] = y.astype(BF16)


def _mlp(x, g, wu, wd, g_next, *, tm, tf, last_layer):
    n, d = x.shape
    dff = wu.shape[1]
    vec = pl.BlockSpec((1, d), lambda i, f: (0, 0))
    row = pl.BlockSpec((tm, d), lambda i, f: (i, 0))
    block_bytes = (2 * _nbytes((tm, d), F32) + 2 * _nbytes((d, tf), BF16) + 2 * _nbytes((tm, d), BF16)
                   + _nbytes((tm, tf), F32))
    f32_out = jax.ShapeDtypeStruct((n, d), F32)
    return pl.pallas_call(
        functools.partial(_mlp_kernel, last_layer=last_layer),
        grid=(n // tm, dff // tf),
        in_specs=[row, vec, pl.BlockSpec((d, tf), lambda i, f: (0, f)),
                  pl.BlockSpec((tf, d), lambda i, f: (f, 0)), vec],
        out_specs=row if last_layer else [row, row],
        out_shape=f32_out if last_layer else [f32_out, jax.ShapeDtypeStruct((n, d), BF16)],
        scratch_shapes=[pltpu.VMEM((tm, d), BF16)],
        compiler_params=pltpu.CompilerParams(
            dimension_semantics=("arbitrary", "arbitrary"), vmem_limit_bytes=_vmem_limit(block_bytes)),
        name="mlp",
    )(x, g, wu, wd, g_next)


def _page_scores(q, k_ref, n_heads, rows):
    page = k_ref.shape[0] // n_heads
    return jnp.concatenate(
        [lax.dot_general(q[h * rows:(h + 1) * rows, :], k_ref[pl.ds(h, page, stride=n_heads), :], _NT,
                         preferred_element_type=F32) for h in range(n_heads)], axis=0)


def _pv_page(v_ref, n_heads, rows):
    page = v_ref.shape[0] // n_heads

    def pv(p):
        return jnp.concatenate(
            [jnp.dot(p[h * rows:(h + 1) * rows, :], v_ref[pl.ds(h, page, stride=n_heads), :],
                     preferred_element_type=F32) for h in range(n_heads)], axis=0)
    return pv


def _new_key_mask(shape, n_new):
    tok_q = lax.broadcasted_iota(jnp.int32, shape, 0) % ROW_PAD
    tok_k = lax.broadcasted_iota(jnp.int32, shape, 1)
    return (tok_k <= tok_q) & (tok_k < n_new)


def _state_refs_store(refs, state):
    for r, s in zip(refs, state):
        r[...] = s


def _dec_diff_kernel(pt_ref, lq1, lk1, lq2, lk2, g_ref, q_ref, kn_ref, vn_ref, kc_ref, vc_ref, o_ref,
                     m_ref, l_ref, acc_ref, *, lam_init, n_new, n_heads):
    del pt_ref
    p = pl.program_id(1)
    rows = 2 * ROW_PAD
    state_refs = (m_ref, l_ref, acc_ref)
    q = q_ref[...]

    @pl.when(p == 0)
    def _():
        s = _page_scores(q, kn_ref, n_heads, rows)
        s = jnp.where(_new_key_mask(s.shape, n_new), s, NEG)
        _state_refs_store(state_refs, _softmax_first(s, _pv_page(vn_ref, n_heads, rows)))

    state = tuple(r[...] for r in state_refs)
    state = _softmax_update(state, _page_scores(q, kc_ref, n_heads, rows), _pv_page(vc_ref, n_heads, rows))
    _state_refs_store(state_refs, state)

    @pl.when(p == pl.num_programs(1) - 1)
    def _():
        lam = _lambda(lq1, lk1, lq2, lk2, lam_init)
        o = acc_ref[...] / l_ref[...]
        for h in range(n_heads):
            o1 = o[h * rows:h * rows + ROW_PAD, :]
            o2 = o[h * rows + ROW_PAD:(h + 1) * rows, :]
            o_ref[h * ROW_PAD:(h + 1) * ROW_PAD, :] = _subln(o1 - lam * o2, g_ref[...], lam_init)


def _dec_diff_attn(page_table, q, k_new, v_new, cache_k, cache_v, lams, g, *, layer, lam_init, n_new):
    b, q_rows, _ = q.shape
    n_pages = page_table.shape[1]
    page_rows = cache_k.shape[2]
    n_heads = q_rows // (2 * ROW_PAD)
    small = pl.BlockSpec((1, DQK_A), lambda bi, p, pt: (0, 0))
    per_b = lambda r: pl.BlockSpec((None, r, HEAD_DIM), lambda bi, p, pt: (bi, 0, 0))
    cache_spec = pl.BlockSpec((None, None, page_rows, HEAD_DIM), lambda bi, p, pt: (layer, pt[bi, p], 0, 0))
    block_bytes = 4 * _nbytes((page_rows, HEAD_DIM), F32)
    return pl.pallas_call(
        functools.partial(_dec_diff_kernel, lam_init=lam_init, n_new=n_new, n_heads=n_heads),
        grid_spec=pltpu.PrefetchScalarGridSpec(
            num_scalar_prefetch=1,
            grid=(b, n_pages),
            in_specs=[small] * 4 + [pl.BlockSpec((1, HEAD_DIM), lambda bi, p, pt: (0, 0)),
                                    per_b(q_rows), per_b(page_rows), per_b(page_rows), cache_spec, cache_spec],
            out_specs=per_b(n_heads * ROW_PAD),
            scratch_shapes=[pltpu.VMEM((q_rows, 1), F32), pltpu.VMEM((q_rows, 1), F32),
                            pltpu.VMEM((q_rows, HEAD_DIM), F32)]),
        out_shape=jax.ShapeDtypeStruct((b, n_heads * ROW_PAD, HEAD_DIM), F32),
        compiler_params=pltpu.CompilerParams(
            dimension_semantics=("arbitrary", "arbitrary"), vmem_limit_bytes=_vmem_limit(block_bytes)),
        name="dec_diff_attn",
    )(page_table, *lams, g, q, k_new, v_new, cache_k, cache_v)


def _dec_kmean_kernel(pt_ref, *refs, n_heads):
    del pt_ref
    o_ref = refs[-1]
    total = sum(jnp.sum(r[...].reshape(-1, n_heads, HEAD_DIM), axis=0) for r in refs[:-1])
    o_ref[...] = total * (1.0 / MOBA_BLOCK)


def _dec_kmean(page_table, cache_k, *, layer, n_heads):
    b, n_pages = page_table.shape
    page_rows = cache_k.shape[2]
    ppb = MOBA_BLOCK // (page_rows // n_heads)
    nblk = n_pages // ppb

    def cache_spec(half):
        return pl.BlockSpec((None, None, page_rows, HEAD_DIM),
                            lambda bi, n, pt: (layer, pt[bi, n * ppb + half], 0, 0))

    return pl.pallas_call(
        functools.partial(_dec_kmean_kernel, n_heads=n_heads),
        grid_spec=pltpu.PrefetchScalarGridSpec(
            num_scalar_prefetch=1,
            grid=(b, nblk),
            in_specs=[cache_spec(half) for half in range(ppb)],
            out_specs=pl.BlockSpec((None, n_heads, HEAD_DIM), lambda bi, n, pt: (bi, n, 0))),
        out_shape=jax.ShapeDtypeStruct((b, nblk * n_heads, HEAD_DIM), F32),
        compiler_params=pltpu.CompilerParams(
            dimension_semantics=("arbitrary", "arbitrary"),
            vmem_limit_bytes=_vmem_limit(ppb * _nbytes((page_rows, HEAD_DIM), F32))),
        name="dec_kmean",
    )(page_table, *([cache_k] * ppb))


def _dec_select_kernel(q_ref, kmean_ref, o_ref, *, n_heads):
    nblk = kmean_ref.shape[0] // n_heads
    blk = lax.broadcasted_iota(jnp.int32, (ROW_PAD, nblk), 1).astype(F32)
    for h in range(n_heads):
        gate = lax.dot_general(q_ref[h * ROW_PAD:(h + 1) * ROW_PAD, :], kmean_ref[pl.ds(h, nblk, stride=n_heads), :],
                               _NT, precision=lax.Precision.HIGHEST, preferred_element_type=F32)
        chosen = jnp.zeros(gate.shape, F32)
        for _ in range(MOBA_TOPK):
            best = jnp.max(gate, axis=-1, keepdims=True)
            pick = jnp.min(jnp.where(gate == best, blk, float(nblk)), axis=-1, keepdims=True)
            chosen = jnp.where(blk == pick, 1.0, chosen)
            gate = jnp.where(blk == pick, -jnp.inf, gate)
        o_ref[h * ROW_PAD:(h + 1) * ROW_PAD, :] = chosen


def _dec_select(q, kmean, *, n_heads):
    b, q_rows, _ = q.shape
    km_rows = kmean.shape[1]
    nblk = km_rows // n_heads
    return pl.pallas_call(
        functools.partial(_dec_select_kernel, n_heads=n_heads),
        grid=(b,),
        in_specs=[pl.BlockSpec((None, q_rows, HEAD_DIM), lambda bi: (bi, 0, 0)),
                  pl.BlockSpec((None, km_rows, HEAD_DIM), lambda bi: (bi, 0, 0))],
        out_specs=pl.BlockSpec((None, q_rows, nblk), lambda bi: (bi, 0, 0)),
        out_shape=jax.ShapeDtypeStruct((b, q_rows, nblk), F32),
        compiler_params=pltpu.CompilerParams(dimension_semantics=("arbitrary",)),
        name="dec_select",
    )(q, kmean)


def _dec_moba_kernel(pt_ref, q_ref, sel_ref, kn_ref, vn_ref, kc_ref, vc_ref, o_ref,
                     m_ref, l_ref, acc_ref, *, n_new, n_heads, ppb):
    del pt_ref
    p = pl.program_id(1)
    rows = ROW_PAD
    state_refs = (m_ref, l_ref, acc_ref)
    q = q_ref[...] * HEAD_DIM ** -0.5

    @pl.when(p == 0)
    def _():
        s = _page_scores(q, kn_ref, n_heads, rows)
        s = jnp.where(_new_key_mask(s.shape, n_new), s, NEG)
        _state_refs_store(state_refs, _softmax_first(s, _pv_page(vn_ref, n_heads, rows)))

    sel = sel_ref[...]
    sel_lane = lax.broadcasted_iota(jnp.int32, sel.shape, 1)
    chosen = jnp.sum(jnp.where(sel_lane == p // ppb, sel, 0.0), axis=-1, keepdims=True)
    s = _page_scores(q, kc_ref, n_heads, rows) + (1.0 - chosen) * NEG
    state = _softmax_update(tuple(r[...] for r in state_refs), s, _pv_page(vc_ref, n_heads, rows))
    _state_refs_store(state_refs, state)

    @pl.when(p == pl.num_programs(1) - 1)
    def _():
        o_ref[...] = acc_ref[...] / l_ref[...]


def _dec_moba_attn(page_table, q, sel, k_new, v_new, cache_k, cache_v, *, layer, n_new, n_heads):
    b, q_rows, _ = q.shape
    n_pages = page_table.shape[1]
    page_rows = cache_k.shape[2]
    ppb = MOBA_BLOCK // (page_rows // n_heads)
    per_b = lambda r, c: pl.BlockSpec((None, r, c), lambda bi, p, pt: (bi, 0, 0))
    cache_spec = pl.BlockSpec((None, None, page_rows, HEAD_DIM), lambda bi, p, pt: (layer, pt[bi, p], 0, 0))
    block_bytes = 4 * _nbytes((page_rows, HEAD_DIM), F32)
    return pl.pallas_call(
        functools.partial(_dec_moba_kernel, n_new=n_new, n_heads=n_heads, ppb=ppb),
        grid_spec=pltpu.PrefetchScalarGridSpec(
            num_scalar_prefetch=1,
            grid=(b, n_pages),
            in_specs=[per_b(q_rows, HEAD_DIM), per_b(q_rows, sel.shape[2]),
                      per_b(page_rows, HEAD_DIM), per_b(page_rows, HEAD_DIM), cache_spec, cache_spec],
            out_specs=per_b(q_rows, HEAD_DIM),
            scratch_shapes=[pltpu.VMEM((q_rows, 1), F32), pltpu.VMEM((q_rows, 1), F32),
                            pltpu.VMEM((q_rows, HEAD_DIM), F32)]),
        out_shape=jax.ShapeDtypeStruct((b, q_rows, HEAD_DIM), F32),
        compiler_params=pltpu.CompilerParams(
            dimension_semantics=("arbitrary", "arbitrary"), vmem_limit_bytes=_vmem_limit(block_bytes)),
        name="dec_moba_attn",
    )(page_table, q, sel, k_new, v_new, cache_k, cache_v)


def _lam_init(layer):
    return 0.8 - 0.6 * math.exp(-0.3 * layer)


def _stack_queries(q, b, t, n_heads):
    q = q.astype(F32).reshape(b, t, n_heads, HEAD_DIM).transpose(0, 2, 1, 3)
    return jnp.pad(q, ((0, 0), (0, 0), (0, ROW_PAD - t), (0, 0)))


def _unstack_heads(o, b, t, n_heads):
    o = o.reshape(b, n_heads, ROW_PAD, HEAD_DIM)[:, :, :t].transpose(0, 2, 1, 3)
    return o.reshape(b * t, n_heads * HEAD_DIM).astype(BF16)


def _new_page(tall, b, t, n_heads, page):
    a = tall.reshape(b, t * n_heads, HEAD_DIM)
    return jnp.pad(a, ((0, 0), (0, (page - t) * n_heads), (0, 0)))


def kernel(x_prompt, x_sample, cache_a_k, cache_a_v, cache_b_k, cache_b_v, page_table, norm_attn, w_in, lambda_q1, lambda_k1, lambda_q2, lambda_k2, subln_a, w_out, norm_mlp, w_up, w_down, norm_final):
    depth = w_in.shape[0]
    bp, tp, d = x_prompt.shape
    bs, ts, _ = x_sample.shape
    n_pool, page = cache_a_k.shape[1], cache_a_k.shape[2]
    past_len = page_table.shape[1] * page
    n_heads = cache_a_k.shape[3]
    assert cache_b_k.shape[3] == n_heads and w_in.shape[2] == N_SECTIONS * n_heads * HEAD_DIM
    assert ts <= ROW_PAD and past_len % MOBA_BLOCK == 0 and past_len // MOBA_BLOCK >= MOBA_TOPK
    assert tp % MOBA_BLOCK == 0 and MOBA_BLOCK % page == 0
    width = n_heads * HEAD_DIM

    w_in_b, w_out_b = w_in.astype(BF16), w_out.astype(BF16)
    w_up_b, w_down_b = w_up.astype(BF16), w_down.astype(BF16)
    caches = tuple(c.reshape(depth, n_pool, page * n_heads, HEAD_DIM)
                   for c in (cache_a_k, cache_a_v, cache_b_k, cache_b_v))

    pos_p = jnp.arange(tp)
    pos_s = jnp.tile(past_len + jnp.arange(ts), bs)
    tabs_p = (_rope_tables(pos_p, ROT_A, DQK_A), _rope_tables(pos_p, ROT_B, HEAD_DIM))
    tabs_s = (_rope_tables(pos_s, ROT_A, DQK_A), _rope_tables(pos_s, ROT_B, HEAD_DIM))
    comp_mask = (jnp.arange(HEAD_DIM)[None, :] // DQK_A == jnp.arange(2)[:, None]).astype(F32)

    tm_p = 256
    tm_s = bs * ts

    def dense_tail(x, oa, ob, l, tm, tf):
        x = _out_proj(x, oa, ob, w_out_b[l], tm=tm)
        last = l == depth - 1
        g_next = norm_final if last else norm_attn[l + 1]
        return _mlp(x, norm_mlp[l][None], w_up_b[l], w_down_b[l], g_next[None], tm=tm, tf=tf, last_layer=last)

    xp = x_prompt.reshape(bp * tp, d)
    xs = x_sample.reshape(bs * ts, d)
    xnp = _norm(xp, norm_attn[0][None], tm=tm_p)
    xns = _norm(xs, norm_attn[0][None], tm=tm_s)
    rows_p, rows_s = [], []
    for l in range(depth):
        lam_init = _lam_init(l)
        lams = tuple(v[l][None] for v in (lambda_q1, lambda_k1, lambda_q2, lambda_k2))
        g_sub = subln_a[l][None]

        qa, ka_w, ka_t, va_w, va_t, qb, kb_w, kb_t, vb_w, vb_t = _proj(xnp, w_in_b[l], *tabs_p, tm=tm_p)
        rows_p.append((ka_t, va_t, kb_t, vb_t))
        as3 = lambda a: a.reshape(bp, tp, width)
        oa = _diff_attn(as3(qa), as3(ka_w), as3(va_w), lams, g_sub, tq=256, lam_init=lam_init)
        ob = _moba_attn(as3(qb), as3(kb_w), as3(vb_w))
        out = dense_tail(xp, oa.reshape(bp * tp, width), ob.reshape(bp * tp, width), l, tm_p, 512)
        xp, xnp = (out, None) if l == depth - 1 else out

        qa, _, ka_t, _, va_t, qb, _, kb_t, _, vb_t = _proj(xns, w_in_b[l], *tabs_s, tm=tm_s)
        rows_s.append((ka_t, va_t, kb_t, vb_t))
        qa_st = _stack_queries(qa, bs, ts, n_heads)[:, :, None] * comp_mask[None, None, :, None, :]
        qa_st = qa_st.reshape(bs, n_heads * 2 * ROW_PAD, HEAD_DIM)
        qb_st = _stack_queries(qb, bs, ts, n_heads).reshape(bs, n_heads * ROW_PAD, HEAD_DIM)
        new_pages = tuple(_new_page(a, bs, ts, n_heads, page) for a in (ka_t, va_t, kb_t, vb_t))
        oa = _dec_diff_attn(page_table, qa_st, new_pages[0], new_pages[1], caches[0], caches[1], lams, g_sub,
                            layer=l, lam_init=lam_init, n_new=ts)
        kmean = _dec_kmean(page_table, caches[2], layer=l, n_heads=n_heads)
        sel = _dec_select(qb_st, kmean, n_heads=n_heads)
        ob = _dec_moba_attn(page_table, qb_st, sel, new_pages[2], new_pages[3], caches[2], caches[3],
                            layer=l, n_new=ts, n_heads=n_heads)
        out = dense_tail(xs, _unstack_heads(oa, bs, ts, n_heads), _unstack_heads(ob, bs, ts, n_heads), l, tm_s, 512)
        xs, xns = (out, None) if l == depth - 1 else out

    def stack(rows, i, b, t):
        return jnp.stack([r[i] for r in rows], axis=0).reshape(depth, b, t, n_heads, HEAD_DIM)

    new_p = tuple(stack(rows_p, i, bp, tp) for i in range(4))
    new_s = tuple(stack(rows_s, i, bs, ts) for i in range(4))
    return (xp.reshape(bp, tp, d), xs.reshape(bs, ts, d)) + new_p + new_s
```

```python
import functools
import math

import jax
import jax.numpy as jnp
from jax import lax
from jax.experimental import pallas as pl
from jax.experimental.pallas import tpu as pltpu

F32 = jnp.float32
BF16 = jnp.bfloat16

HEAD_DIM = 128
DQK_A = HEAD_DIM // 2
ROT_A = DQK_A // 4
ROT_B = HEAD_DIM // 4
ROPE_THETA = 500000.0
MOBA_BLOCK = 256
MOBA_TOPK = 3
NORM_EPS = 1e-5
NEG = -1e30
N_SECTIONS = 6
ROW_PAD = 8
V7X_VMEM_BYTES = 64 * 1024 * 1024

TM_DENSE = 512
TF_MLP = 1024
TQ_ATTN = 512
DEC_PAGES_PER_STEP = 4

_NT = (((1,), (1,)), ((), ()))


def _vmem_limit(block_bytes):
    want = 2 * block_bytes + 8 * 1024 * 1024
    return int(min(want, V7X_VMEM_BYTES - 8 * 1024 * 1024))


def _nbytes(shape, dtype):
    return math.prod(shape) * jnp.dtype(dtype).itemsize


def _rmsnorm(x, g):
    return x * lax.rsqrt(jnp.mean(x * x, axis=-1, keepdims=True) + NORM_EPS) * g


def _layer_vec(width, layer):
    return pl.BlockSpec((None, 1, width), lambda *_: (layer, 0, 0))


def _rope_tables(pos, rot, comp_width):
    half = rot // 2
    freqs = jnp.exp(-math.log(ROPE_THETA) * jnp.arange(half, dtype=F32) / half)
    ang = pos.astype(F32)[:, None] * freqs[None, :]
    cos, sin = jnp.cos(ang), jnp.sin(ang)
    n = pos.shape[0]
    rest = comp_width - rot
    zeros_h = jnp.zeros((n, half), F32)
    c = jnp.concatenate([cos, cos, jnp.ones((n, rest), F32)], axis=-1)
    s_next = jnp.concatenate([-sin, zeros_h, jnp.zeros((n, rest), F32)], axis=-1)
    s_prev = jnp.concatenate([zeros_h, sin, jnp.zeros((n, rest), F32)], axis=-1)
    reps = HEAD_DIM // comp_width
    return tuple(jnp.tile(t, (1, reps)) for t in (c, s_next, s_prev))


def _tile_cast_kernel(x_ref, o_ref):
    o_ref[...] = x_ref[...].astype(o_ref.dtype)


def _tile_columns_bf16(w, tn):
    depth, d, n = w.shape
    return pl.pallas_call(
        _tile_cast_kernel,
        grid=(depth, n // tn),
        in_specs=[pl.BlockSpec((None, d, tn), lambda l, j: (l, 0, j))],
        out_specs=pl.BlockSpec((None, None, d, tn), lambda l, j: (l, j, 0, 0)),
        out_shape=jax.ShapeDtypeStruct((depth, n // tn, d, tn), BF16),
        compiler_params=pltpu.CompilerParams(
            dimension_semantics=("arbitrary", "arbitrary"),
            vmem_limit_bytes=_vmem_limit(_nbytes((d, tn), F32) + _nbytes((d, tn), BF16))),
        name="tile_cast",
    )(w)


def _norm_kernel(x_ref, g_ref, o_ref):
    o_ref[...] = _rmsnorm(x_ref[...], g_ref[...]).astype(o_ref.dtype)


def _norm(x, g, layer, *, tm):
    n, d = x.shape
    row = pl.BlockSpec((tm, d), lambda i: (i, 0))
    return pl.pallas_call(
        _norm_kernel,
        grid=(n // tm,),
        in_specs=[row, _layer_vec(d, layer)],
        out_specs=row,
        out_shape=jax.ShapeDtypeStruct((n, d), BF16),
        compiler_params=pltpu.CompilerParams(
            dimension_semantics=("arbitrary",),
            vmem_limit_bytes=_vmem_limit(_nbytes((tm, d), F32) + _nbytes((tm, d), BF16))),
        name="norm",
    )(x, g)


def _store_heads(y, tabs, half, scale, wide_ref, tall_ref):
    tm = y.shape[0]
    n_heads = y.shape[1] // HEAD_DIM
    if tabs is not None:
        c, s_next, s_prev = (t[...] for t in tabs)
    for h in range(n_heads):
        sl = slice(h * HEAD_DIM, (h + 1) * HEAD_DIM)
        r = y[:, sl]
        if tabs is not None:
            r = (r * c + pltpu.roll(r, HEAD_DIM - half, 1) * s_next + pltpu.roll(r, half, 1) * s_prev)
        if tall_ref is not None:
            tall_ref[pl.ds(h, tm, stride=n_heads), :] = r
        if wide_ref is not None:
            wide_ref[:, sl] = (r * scale if scale != 1.0 else r).astype(wide_ref.dtype)


def _proj_kernel(*refs, n_prev):
    xn_ref, w_ref, ca, sna, spa, cb, snb, spb = refs[:8]
    qa_ref, ka_w, ka_t, va_w, va_t, qb_ref, kb_w, kb_t, vb_w, vb_t = refs[8 + n_prev:]
    j = pl.program_id(0)
    y = jnp.dot(xn_ref[...], w_ref[...], preferred_element_type=F32)
    tabs_a, tabs_b = (ca, sna, spa), (cb, snb, spb)
    sections = (
        (tabs_a, ROT_A // 2, DQK_A ** -0.5, qa_ref, None),
        (tabs_a, ROT_A // 2, 1.0, ka_w, ka_t),
        (None, 0, 1.0, va_w, va_t),
        (tabs_b, ROT_B // 2, 1.0, qb_ref, None),
        (tabs_b, ROT_B // 2, 1.0, kb_w, kb_t),
        (None, 0, 1.0, vb_w, vb_t),
    )
    for s, args in enumerate(sections):
        pl.when(j == s)(functools.partial(_store_heads, y, *args))


def _proj(xn, w, layer, tabs_a, tabs_b, prev_tall, *, tm):
    n, d = xn.shape
    depth, _, _, sec = w.shape
    n_heads = sec // HEAD_DIM
    n_i = n // tm
    n_tab = tabs_a[0].shape[0] // tm
    tab_spec = pl.BlockSpec((tm, HEAD_DIM), lambda j, i: (i % n_tab, 0))

    def park(section, j, i):
        return jnp.where(j < section, 0, jnp.where(j > section, n_i - 1, i))

    def wide_spec(section):
        return pl.BlockSpec((tm, sec), lambda j, i: (park(section, j, i), 0))

    def tall_spec(section):
        return pl.BlockSpec((None, tm * n_heads, HEAD_DIM), lambda j, i: (layer, park(section, j, i), 0))

    wide_sds = lambda dt: jax.ShapeDtypeStruct((n, sec), dt)
    tall_sds = jax.ShapeDtypeStruct((depth, n * n_heads, HEAD_DIM), F32)
    out_specs = [wide_spec(0)]
    out_shape = [wide_sds(BF16)]
    tall_out_idx = []
    for section in (1, 2, 3, 4, 5):
        if section == 3:
            out_specs.append(wide_spec(3))
            out_shape.append(wide_sds(F32))
            continue
        tall_out_idx.append(len(out_specs) + 1)
        out_specs += [wide_spec(section), tall_spec(section)]
        out_shape += [wide_sds(BF16), tall_sds]
    n_in = 8
    aliases = {n_in + k: tall_out_idx[k] for k in range(len(prev_tall))}
    block_bytes = (_nbytes((tm, d), BF16) + _nbytes((d, sec), BF16) + 6 * _nbytes((tm, HEAD_DIM), F32)
                   + 5 * _nbytes((tm, sec), BF16) + 5 * _nbytes((tm, sec), F32))
    return pl.pallas_call(
        functools.partial(_proj_kernel, n_prev=len(prev_tall)),
        grid=(N_SECTIONS, n_i),
        in_specs=[pl.BlockSpec((tm, d), lambda j, i: (i, 0)),
                  pl.BlockSpec((None, None, d, sec), lambda j, i: (layer, j, 0, 0))] + [tab_spec] * 6
                 + [pl.BlockSpec(memory_space=pl.ANY)] * len(prev_tall),
        out_specs=out_specs,
        out_shape=out_shape,
        input_output_aliases=aliases,
        compiler_params=pltpu.CompilerParams(
            dimension_semantics=("arbitrary", "arbitrary"),
            vmem_limit_bytes=_vmem_limit(block_bytes)),
        name="proj",
    )(xn, w, *tabs_a, *tabs_b, *prev_tall)


def _lambda(lq1, lk1, lq2, lk2, lam_init):
    a = jnp.exp(jnp.sum(lq1[...] * lk1[...], axis=-1, keepdims=True))
    b = jnp.exp(jnp.sum(lq2[...] * lk2[...], axis=-1, keepdims=True))
    return a - b + lam_init


def _split_components(q):
    lane = lax.broadcasted_iota(jnp.int32, q.shape, 1)
    zero = jnp.zeros_like(q)
    return jnp.where(lane < DQK_A, q, zero), jnp.where(lane >= DQK_A, q, zero)


def _scores(q, k):
    return lax.dot_general(q, k, _NT, preferred_element_type=F32)


def _softmax_first(s, v):
    m = jnp.max(s, axis=-1, keepdims=True)
    p = jnp.exp(s - m)
    return m, jnp.sum(p, axis=-1, keepdims=True), jnp.dot(p.astype(BF16), v, preferred_element_type=F32)


def _softmax_update(state, s, v):
    m, l, acc = state
    m_new = jnp.maximum(m, jnp.max(s, axis=-1, keepdims=True))
    alpha = jnp.exp(m - m_new)
    p = jnp.exp(s - m_new)
    return (m_new, alpha * l + jnp.sum(p, axis=-1, keepdims=True),
            alpha * acc + jnp.dot(p.astype(BF16), v, preferred_element_type=F32))


def _subln(o, g, lam_init):
    return _rmsnorm(o, g) * (1.0 - lam_init)


def _causal(tq):
    return (lax.broadcasted_iota(jnp.int32, (tq, tq), 1) <= lax.broadcasted_iota(jnp.int32, (tq, tq), 0))


def _diff_attn_kernel(lq1, lk1, lq2, lk2, g_ref, q_ref, k_ref, v_ref, o_ref, *, tq, lam_init):
    qi = pl.program_id(2)
    q1, q2 = _split_components(q_ref[...])

    def kv(j):
        start = pl.multiple_of(j * tq, tq)
        return k_ref[pl.ds(start, tq), :], v_ref[pl.ds(start, tq), :]

    k, v = kv(qi)
    causal = _causal(tq)
    st1 = _softmax_first(jnp.where(causal, _scores(q1, k), NEG), v)
    st2 = _softmax_first(jnp.where(causal, _scores(q2, k), NEG), v)

    def body(j, carry):
        st1, st2 = carry
        k, v = kv(j)
        return _softmax_update(st1, _scores(q1, k), v), _softmax_update(st2, _scores(q2, k), v)

    (_, l1, a1), (_, l2, a2) = lax.fori_loop(0, qi, body, (st1, st2))
    lam = _lambda(lq1, lk1, lq2, lk2, lam_init)
    o = a1 / l1 - lam * (a2 / l2)
    o_ref[...] = _subln(o, g_ref[...], lam_init).astype(o_ref.dtype)


def _diff_attn(q, k, v, lams, g, layer, *, tq, lam_init):
    b, t, w = q.shape
    kv_spec = pl.BlockSpec((None, t, HEAD_DIM), lambda bi, h, qi: (bi, 0, h))
    q_spec = pl.BlockSpec((None, tq, HEAD_DIM), lambda bi, h, qi: (bi, qi, h))
    block_bytes = 2 * _nbytes((t, HEAD_DIM), BF16) + 2 * _nbytes((tq, HEAD_DIM), BF16) + 4 * _nbytes((tq, tq), F32)
    return pl.pallas_call(
        functools.partial(_diff_attn_kernel, tq=tq, lam_init=lam_init),
        grid=(b, w // HEAD_DIM, t // tq),
        in_specs=[_layer_vec(DQK_A, layer)] * 4 + [_layer_vec(HEAD_DIM, layer), q_spec, kv_spec, kv_spec],
        out_specs=q_spec,
        out_shape=jax.ShapeDtypeStruct((b, t, w), BF16),
        compiler_params=pltpu.CompilerParams(
            dimension_semantics=("arbitrary",) * 3, vmem_limit_bytes=_vmem_limit(block_bytes)),
        name="diff_attn",
    )(*lams, g, q, k, v)


def _allowed_blocks(gate_t, own, nb):
    blk = lax.broadcasted_iota(jnp.int32, gate_t.shape, 0)
    past = blk < own
    gate_t = jnp.where(past, gate_t, -jnp.inf)
    rank = jnp.zeros(gate_t.shape, F32)
    for m in range(nb):
        gm = gate_t[m:m + 1, :]
        ahead = (gm > gate_t) | ((gm == gate_t) & (m < blk))
        rank = rank + jnp.where(ahead, 1.0, 0.0)
    return jnp.where((past & (rank < MOBA_TOPK)) | (blk == own), 1.0, 0.0)


def _moba_kernel(q_ref, k_ref, v_ref, o_ref, kmean_ref, *, nb, tq):
    qi = pl.program_id(2)
    bpt = tq // MOBA_BLOCK

    @pl.when(qi == 0)
    def _():
        k = k_ref[...].astype(F32)
        kmean_ref[...] = jnp.zeros(kmean_ref.shape, F32)
        kmean_ref[:nb, :] = jnp.mean(k.reshape(nb, MOBA_BLOCK, HEAD_DIM), axis=1)

    q = q_ref[...]
    gate_t = lax.dot_general(kmean_ref[...], q, _NT, precision=lax.Precision.HIGHEST,
                             preferred_element_type=F32)
    own = qi * bpt + lax.broadcasted_iota(jnp.int32, gate_t.shape, 1) // MOBA_BLOCK
    allowed_t = _allowed_blocks(gate_t, own, nb).astype(BF16)
    allowed_t = jnp.concatenate(
        [allowed_t, jnp.zeros((HEAD_DIM - allowed_t.shape[0], tq), BF16)], axis=0)
    eye = (lax.broadcasted_iota(jnp.int32, (tq, tq), 0)
           == lax.broadcasted_iota(jnp.int32, (tq, tq), 1)).astype(BF16)
    allowed = lax.dot_general(eye, allowed_t, _NT, preferred_element_type=F32)
    blk_lane = lax.broadcasted_iota(jnp.int32, allowed.shape, 1)

    qs = (q * HEAD_DIM ** -0.5).astype(BF16)

    def kv(j):
        start = pl.multiple_of(j * tq, tq)
        return k_ref[pl.ds(start, tq), :], v_ref[pl.ds(start, tq), :]

    def block_scores(j, k):
        s = _scores(qs, k)
        parts = []
        for i in range(bpt):
            ok = jnp.sum(jnp.where(blk_lane == j * bpt + i, allowed, 0.0), axis=-1, keepdims=True)
            parts.append(s[:, i * MOBA_BLOCK:(i + 1) * MOBA_BLOCK] + (1.0 - ok) * NEG)
        return jnp.concatenate(parts, axis=1)

    k, v = kv(qi)
    state = _softmax_first(jnp.where(_causal(tq), block_scores(qi, k), NEG), v)

    def body(j, state):
        k, v = kv(j)
        return _softmax_update(state, block_scores(j, k), v)

    _, l, acc = lax.fori_loop(0, qi, body, state)
    o_ref[...] = (acc / l).astype(o_ref.dtype)


def _moba_attn(q, k, v, *, tq):
    b, t, w = q.shape
    nb = t // MOBA_BLOCK
    nb_pad = -(-nb // 8) * 8
    kv_spec = pl.BlockSpec((None, t, HEAD_DIM), lambda bi, h, qi: (bi, 0, h))
    q_spec = pl.BlockSpec((None, tq, HEAD_DIM), lambda bi, h, qi: (bi, qi, h))
    block_bytes = 2 * _nbytes((t, HEAD_DIM), BF16) + 2 * _nbytes((tq, HEAD_DIM), F32) + 4 * _nbytes((tq, tq), F32)
    return pl.pallas_call(
        functools.partial(_moba_kernel, nb=nb, tq=tq),
        grid=(b, w // HEAD_DIM, t // tq),
        in_specs=[q_spec, kv_spec, kv_spec],
        out_specs=q_spec,
        out_shape=jax.ShapeDtypeStruct((b, t, w), BF16),
        scratch_shapes=[pltpu.VMEM((nb_pad, HEAD_DIM), F32)],
        compiler_params=pltpu.CompilerParams(
            dimension_semantics=("arbitrary",) * 3, vmem_limit_bytes=_vmem_limit(block_bytes)),
        name="moba_attn",
    )(q, k, v)


def _out_proj_kernel(x_ref, oa_ref, ob_ref, w_ref, o_ref):
    wa = oa_ref.shape[1]
    o_ref[...] = (x_ref[...]
                  + jnp.dot(oa_ref[...], w_ref[:wa, :], preferred_element_type=F32)
                  + jnp.dot(ob_ref[...], w_ref[wa:, :], preferred_element_type=F32))


def _out_proj(x, oa, ob, w, layer, *, tm):
    n, d = x.shape
    wa, wb = oa.shape[1], ob.shape[1]
    row = lambda width: pl.BlockSpec((tm, width), lambda i: (i, 0))
    block_bytes = (2 * _nbytes((tm, d), F32) + _nbytes((tm, wa + wb), BF16) + _nbytes(w.shape[1:], BF16))
    return pl.pallas_call(
        _out_proj_kernel,
        grid=(n // tm,),
        in_specs=[row(d), row(wa), row(wb), pl.BlockSpec((None,) + w.shape[1:], lambda i: (layer, 0, 0))],
        out_specs=row(d),
        out_shape=jax.ShapeDtypeStruct((n, d), F32),
        compiler_params=pltpu.CompilerParams(
            dimension_semantics=("arbitrary",), vmem_limit_bytes=_vmem_limit(block_bytes)),
        name="out_proj",
    )(x, oa, ob, w)


def _mlp_kernel(x_ref, g_ref, wu_ref, wd_ref, gn_ref, o_ref, *scratch_or_out, last_layer):
    if last_layer:
        (xn_ref,) = scratch_or_out
    else:
        xn_out_ref, xn_ref = scratch_or_out
    f = pl.program_id(1)

    @pl.when(f == 0)
    def _():
        x = x_ref[...]
        xn_ref[...] = _rmsnorm(x, g_ref[...]).astype(BF16)
        o_ref[...] = x

    h = jnp.dot(xn_ref[...], wu_ref[...], preferred_element_type=F32)
    h = jnp.square(jnp.maximum(h, 0.0)).astype(BF16)
    o_ref[...] += jnp.dot(h, wd_ref[...], preferred_element_type=F32)

    @pl.when(f == pl.num_programs(1) - 1)
    def _():
        y = _rmsnorm(o_ref[...], gn_ref[...])
        if last_layer:
            o_ref[...] = y
        else:
            xn_out_ref[...] = y.astype(BF16)


def _mlp(x, g, wu, wd, g_next, layer, next_row, *, tm, last_layer):
    n, d = x.shape
    _, n_f, _, tf = wu.shape
    row = pl.BlockSpec((tm, d), lambda i, f: (i, 0))
    block_bytes = (2 * _nbytes((tm, d), F32) + 2 * _nbytes((d, tf), BF16) + 2 * _nbytes((tm, d), BF16)
                   + _nbytes((tm, tf), F32))
    f32_out = jax.ShapeDtypeStruct((n, d), F32)
    return pl.pallas_call(
        functools.partial(_mlp_kernel, last_layer=last_layer),
        grid=(n // tm, n_f),
        in_specs=[row, _layer_vec(d, layer),
                  pl.BlockSpec((None, None, d, tf), lambda i, f: (layer, f, 0, 0)),
                  pl.BlockSpec((None, tf, d), lambda i, f: (layer, f, 0)),
                  _layer_vec(d, next_row)],
        out_specs=row if last_layer else [row, row],
        out_shape=f32_out if last_layer else [f32_out, jax.ShapeDtypeStruct((n, d), BF16)],
        scratch_shapes=[pltpu.VMEM((tm, d), BF16)],
        compiler_params=pltpu.CompilerParams(
            dimension_semantics=("arbitrary", "arbitrary"), vmem_limit_bytes=_vmem_limit(block_bytes)),
        name="mlp",
    )(x, g, wu, wd, g_next)


def _head_bias(q_rows, rows, cols, n_heads):
    r = lax.broadcasted_iota(jnp.int32, (q_rows, cols), 0)
    c = lax.broadcasted_iota(jnp.int32, (q_rows, cols), 1)
    return jnp.where(c % n_heads == r // rows, 0.0, NEG)


def _new_key_ok(q_rows, cols, n_heads, n_new):
    tok_q = lax.broadcasted_iota(jnp.int32, (q_rows, cols), 0) % ROW_PAD
    tok_k = lax.broadcasted_iota(jnp.int32, (q_rows, cols), 1) // n_heads
    return (tok_k <= tok_q) & (tok_k < n_new)


def _page_attn_step(q, kn_ref, vn_ref, k_refs, v_refs, bias_ref, state_refs, *, rows, n_heads, n_new, penalty):
    p = pl.program_id(1)
    q_rows = q.shape[0]
    page_rows = kn_ref.shape[0]

    @pl.when(p == 0)
    def _():
        bias = _head_bias(q_rows, rows, page_rows, n_heads)
        bias_ref[...] = bias
        s = _scores(q, kn_ref[...].astype(BF16)) + bias
        s = jnp.where(_new_key_ok(q_rows, page_rows, n_heads, n_new), s, NEG)
        for r, val in zip(state_refs, _softmax_first(s, vn_ref[...].astype(BF16))):
            r[...] = val

    bias = bias_ref[...]
    parts = []
    for i, k in enumerate(k_refs):
        part = _scores(q, k[...].astype(BF16)) + bias
        parts.append(part if penalty is None else part + penalty(i))
    s = jnp.concatenate(parts, axis=1)
    v = jnp.concatenate([v[...].astype(BF16) for v in v_refs], axis=0)
    state = _softmax_update(tuple(r[...] for r in state_refs), s, v)
    for r, val in zip(state_refs, state):
        r[...] = val


def _dec_diff_kernel(pt_ref, lq1, lk1, lq2, lk2, g_ref, q_ref, kn_ref, vn_ref, *refs,
                     lam_init, n_new, n_heads, pps):
    del pt_ref
    k_refs, v_refs, o_ref = refs[:pps], refs[pps:2 * pps], refs[2 * pps]
    bias_ref, m_ref, l_ref, acc_ref = refs[2 * pps + 1:]
    rows = 2 * ROW_PAD
    _page_attn_step(q_ref[...].astype(BF16), kn_ref, vn_ref, k_refs, v_refs, bias_ref, (m_ref, l_ref, acc_ref),
                    rows=rows, n_heads=n_heads, n_new=n_new, penalty=None)

    @pl.when(pl.program_id(1) == pl.num_programs(1) - 1)
    def _():
        lam = _lambda(lq1, lk1, lq2, lk2, lam_init)
        o = acc_ref[...] / l_ref[...]
        for h in range(n_heads):
            o1 = o[h * rows:h * rows + ROW_PAD, :]
            o2 = o[h * rows + ROW_PAD:(h + 1) * rows, :]
            o_ref[h * ROW_PAD:(h + 1) * ROW_PAD, :] = _subln(o1 - lam * o2, g_ref[...], lam_init)


def _paged_specs(page_table, cache, layer, pps):
    page_rows = cache.shape[2]

    def spec(i):
        return pl.BlockSpec((None, None, page_rows, HEAD_DIM),
                            lambda bi, p, pt: (layer, pt[bi, p * pps + i], 0, 0))
    return [spec(i) for i in range(pps)]


def _dec_diff_attn(page_table, q, k_new, v_new, cache_k, cache_v, lams, g, *, layer, lam_init, n_new, pps):
    b, q_rows, _ = q.shape
    n_pages = page_table.shape[1]
    page_rows = cache_k.shape[2]
    n_heads = q_rows // (2 * ROW_PAD)
    vec = lambda width: pl.BlockSpec((None, 1, width), lambda bi, p, pt: (layer, 0, 0))
    per_b = lambda r: pl.BlockSpec((None, r, HEAD_DIM), lambda bi, p, pt: (bi, 0, 0))
    block_bytes = (2 + 2 * pps) * _nbytes((page_rows, HEAD_DIM), F32) + 4 * pps * _nbytes((q_rows, page_rows), F32)
    return pl.pallas_call(
        functools.partial(_dec_diff_kernel, lam_init=lam_init, n_new=n_new, n_heads=n_heads, pps=pps),
        grid_spec=pltpu.PrefetchScalarGridSpec(
            num_scalar_prefetch=1,
            grid=(b, n_pages // pps),
            in_specs=[vec(DQK_A)] * 4 + [vec(HEAD_DIM), per_b(q_rows), per_b(page_rows), per_b(page_rows)]
                     + _paged_specs(page_table, cache_k, layer, pps) + _paged_specs(page_table, cache_v, layer, pps),
            out_specs=per_b(n_heads * ROW_PAD),
            scratch_shapes=[pltpu.VMEM((q_rows, page_rows), F32), pltpu.VMEM((q_rows, 1), F32),
                            pltpu.VMEM((q_rows, 1), F32), pltpu.VMEM((q_rows, HEAD_DIM), F32)]),
        out_shape=jax.ShapeDtypeStruct((b, n_heads * ROW_PAD, HEAD_DIM), F32),
        compiler_params=pltpu.CompilerParams(
            dimension_semantics=("arbitrary", "arbitrary"), vmem_limit_bytes=_vmem_limit(block_bytes)),
        name="dec_diff_attn",
    )(page_table, *lams, g, q, k_new, v_new, *([cache_k] * pps), *([cache_v] * pps))


def _dec_kmean_kernel(pt_ref, *refs, n_heads, ppb):
    del pt_ref
    o_ref = refs[-1]
    sums = [jnp.sum(r[...].reshape(-1, n_heads, HEAD_DIM), axis=0) for r in refs[:-1]]
    for i in range(len(sums) // ppb):
        o_ref[i * n_heads:(i + 1) * n_heads, :] = sum(sums[i * ppb:(i + 1) * ppb]) * (1.0 / MOBA_BLOCK)


def _dec_kmean(page_table, cache_k, *, layer, n_heads, pps):
    b, n_pages = page_table.shape
    page_rows = cache_k.shape[2]
    ppb = MOBA_BLOCK // (page_rows // n_heads)
    nblk = n_pages // ppb
    bps = pps // ppb
    return pl.pallas_call(
        functools.partial(_dec_kmean_kernel, n_heads=n_heads, ppb=ppb),
        grid_spec=pltpu.PrefetchScalarGridSpec(
            num_scalar_prefetch=1,
            grid=(b, n_pages // pps),
            in_specs=_paged_specs(page_table, cache_k, layer, pps),
            out_specs=pl.BlockSpec((None, bps * n_heads, HEAD_DIM), lambda bi, n, pt: (bi, n, 0))),
        out_shape=jax.ShapeDtypeStruct((b, nblk * n_heads, HEAD_DIM), F32),
        compiler_params=pltpu.CompilerParams(
            dimension_semantics=("arbitrary", "arbitrary"),
            vmem_limit_bytes=_vmem_limit(pps * _nbytes((page_rows, HEAD_DIM), F32))),
        name="dec_kmean",
    )(page_table, *([cache_k] * pps))


def _dec_select_kernel(q_ref, kmean_ref, o_ref, *, n_heads):
    nblk = kmean_ref.shape[0] // n_heads
    blk = lax.broadcasted_iota(jnp.int32, (ROW_PAD, nblk), 1).astype(F32)
    for h in range(n_heads):
        gate = lax.dot_general(q_ref[h * ROW_PAD:(h + 1) * ROW_PAD, :], kmean_ref[pl.ds(h, nblk, stride=n_heads), :],
                               _NT, precision=lax.Precision.HIGHEST, preferred_element_type=F32)
        chosen = jnp.zeros(gate.shape, F32)
        for _ in range(MOBA_TOPK):
            best = jnp.max(gate, axis=-1, keepdims=True)
            pick = jnp.min(jnp.where(gate == best, blk, float(nblk)), axis=-1, keepdims=True)
            chosen = jnp.where(blk == pick, 1.0, chosen)
            gate = jnp.where(blk == pick, -jnp.inf, gate)
        o_ref[h * ROW_PAD:(h + 1) * ROW_PAD, :] = chosen


def _dec_select(q, kmean, *, n_heads):
    b, q_rows, _ = q.shape
    km_rows = kmean.shape[1]
    nblk = km_rows // n_heads
    return pl.pallas_call(
        functools.partial(_dec_select_kernel, n_heads=n_heads),
        grid=(b,),
        in_specs=[pl.BlockSpec((None, q_rows, HEAD_DIM), lambda bi: (bi, 0, 0)),
                  pl.BlockSpec((None, km_rows, HEAD_DIM), lambda bi: (bi, 0, 0))],
        out_specs=pl.BlockSpec((None, q_rows, nblk), lambda bi: (bi, 0, 0)),
        out_shape=jax.ShapeDtypeStruct((b, q_rows, nblk), F32),
        compiler_params=pltpu.CompilerParams(dimension_semantics=("arbitrary",)),
        name="dec_select",
    )(q, kmean)


def _dec_moba_kernel(pt_ref, q_ref, sel_ref, kn_ref, vn_ref, *refs, n_new, n_heads, pps, ppb):
    del pt_ref
    k_refs, v_refs, o_ref = refs[:pps], refs[pps:2 * pps], refs[2 * pps]
    bias_ref, m_ref, l_ref, acc_ref = refs[2 * pps + 1:]
    sel = sel_ref[...]
    blk_lane = lax.broadcasted_iota(jnp.int32, sel.shape, 1)
    first_blk = pl.program_id(1) * (pps // ppb)
    penalties = []
    for i in range(pps // ppb):
        chosen = jnp.sum(jnp.where(blk_lane == first_blk + i, sel, 0.0), axis=-1, keepdims=True)
        penalties.append((1.0 - chosen) * NEG)
    q = (q_ref[...] * HEAD_DIM ** -0.5).astype(BF16)
    _page_attn_step(q, kn_ref, vn_ref, k_refs, v_refs, bias_ref, (m_ref, l_ref, acc_ref),
                    rows=ROW_PAD, n_heads=n_heads, n_new=n_new, penalty=lambda i: penalties[i // ppb])

    @pl.when(pl.program_id(1) == pl.num_programs(1) - 1)
    def _():
        o_ref[...] = acc_ref[...] / l_ref[...]


def _dec_moba_attn(page_table, q, sel, k_new, v_new, cache_k, cache_v, *, layer, n_new, n_heads, pps):
    b, q_rows, _ = q.shape
    n_pages = page_table.shape[1]
    page_rows = cache_k.shape[2]
    ppb = MOBA_BLOCK // (page_rows // n_heads)
    per_b = lambda r, c: pl.BlockSpec((None, r, c), lambda bi, p, pt: (bi, 0, 0))
    block_bytes = (2 + 2 * pps) * _nbytes((page_rows, HEAD_DIM), F32) + 4 * pps * _nbytes((q_rows, page_rows), F32)
    return pl.pallas_call(
        functools.partial(_dec_moba_kernel, n_new=n_new, n_heads=n_heads, pps=pps, ppb=ppb),
        grid_spec=pltpu.PrefetchScalarGridSpec(
            num_scalar_prefetch=1,
            grid=(b, n_pages // pps),
            in_specs=[per_b(q_rows, HEAD_DIM), per_b(q_rows, sel.shape[2]),
                      per_b(page_rows, HEAD_DIM), per_b(page_rows, HEAD_DIM)]
                     + _paged_specs(page_table, cache_k, layer, pps) + _paged_specs(page_table, cache_v, layer, pps),
            out_specs=per_b(q_rows, HEAD_DIM),
            scratch_shapes=[pltpu.VMEM((q_rows, page_rows), F32), pltpu.VMEM((q_rows, 1), F32),
                            pltpu.VMEM((q_rows, 1), F32), pltpu.VMEM((q_rows, HEAD_DIM), F32)]),
        out_shape=jax.ShapeDtypeStruct((b, q_rows, HEAD_DIM), F32),
        compiler_params=pltpu.CompilerParams(
            dimension_semantics=("arbitrary", "arbitrary"), vmem_limit_bytes=_vmem_limit(block_bytes)),
        name="dec_moba_attn",
    )(page_table, q, sel, k_new, v_new, *([cache_k] * pps), *([cache_v] * pps))


def _lam_init(layer):
    return 0.8 - 0.6 * math.exp(-0.3 * layer)


def _stack_queries(q, b, t, n_heads):
    q = q.astype(F32).reshape(b, t, n_heads, HEAD_DIM).transpose(0, 2, 1, 3)
    return jnp.pad(q, ((0, 0), (0, 0), (0, ROW_PAD - t), (0, 0)))


def _unstack_heads(o, b, t, n_heads):
    o = o.reshape(b, n_heads, ROW_PAD, HEAD_DIM)[:, :, :t].transpose(0, 2, 1, 3)
    return o.reshape(b * t, n_heads * HEAD_DIM).astype(BF16)


def _new_page(tall, b, t, n_heads, page):
    a = tall.reshape(b, t * n_heads, HEAD_DIM)
    return jnp.pad(a, ((0, 0), (0, (page - t) * n_heads), (0, 0)))


def kernel(x_prompt, x_sample, cache_a_k, cache_a_v, cache_b_k, cache_b_v, page_table, norm_attn, w_in, lambda_q1, lambda_k1, lambda_q2, lambda_k2, subln_a, w_out, norm_mlp, w_up, w_down, norm_final):
    depth, d, d_in = w_in.shape
    dff = w_up.shape[2]
    bp, tp, _ = x_prompt.shape
    bs, ts, _ = x_sample.shape
    n_pool, page = cache_a_k.shape[1], cache_a_k.shape[2]
    n_pages = page_table.shape[1]
    past_len = n_pages * page
    n_heads = cache_a_k.shape[3]
    sec = n_heads * HEAD_DIM
    ppb = MOBA_BLOCK // page
    assert cache_b_k.shape[3] == n_heads and d_in == N_SECTIONS * sec
    assert ts <= ROW_PAD and past_len % MOBA_BLOCK == 0 and past_len // MOBA_BLOCK >= MOBA_TOPK
    pps = DEC_PAGES_PER_STEP
    assert MOBA_BLOCK % page == 0 and n_pages % pps == 0 and pps % ppb == 0
    tm_p = min(TM_DENSE, bp * tp)
    tq = min(TQ_ATTN, tp)
    tf = min(TF_MLP, dff)
    assert tp % tq == 0 and tq % MOBA_BLOCK == 0 and (bp * tp) % tm_p == 0 and tp % tm_p == 0 and dff % tf == 0

    w_in_b = _tile_columns_bf16(w_in, sec)
    w_up_b = _tile_columns_bf16(w_up, tf)
    w_out_b, w_down_b = w_out.astype(BF16), w_down.astype(BF16)
    caches = tuple(c.reshape(depth, n_pool, page * n_heads, HEAD_DIM)
                   for c in (cache_a_k, cache_a_v, cache_b_k, cache_b_v))
    norm_attn3, norm_mlp3, subln3 = norm_attn[:, None], norm_mlp[:, None], subln_a[:, None]
    norm_final3 = norm_final[None, None]
    lams = tuple(v[:, None] for v in (lambda_q1, lambda_k1, lambda_q2, lambda_k2))

    pos_p = jnp.arange(tp)
    pos_s = jnp.tile(past_len + jnp.arange(ts), bs)
    tabs_p = (_rope_tables(pos_p, ROT_A, DQK_A), _rope_tables(pos_p, ROT_B, HEAD_DIM))
    tabs_s = (_rope_tables(pos_s, ROT_A, DQK_A), _rope_tables(pos_s, ROT_B, HEAD_DIM))
    comp_mask = (jnp.arange(HEAD_DIM)[None, :] // DQK_A == jnp.arange(2)[:, None]).astype(F32)

    tm_s = bs * ts

    def dense_tail(x, oa, ob, l, tm):
        x = _out_proj(x, oa, ob, w_out_b, l, tm=tm)
        last = l == depth - 1
        g_next, next_row = (norm_final3, 0) if last else (norm_attn3, l + 1)
        return _mlp(x, norm_mlp3, w_up_b, w_down_b, g_next, l, next_row, tm=tm, last_layer=last)

    xp = x_prompt.reshape(bp * tp, d)
    xs = x_sample.reshape(bs * ts, d)
    xnp = _norm(xp, norm_attn3, 0, tm=tm_p)
    xns = _norm(xs, norm_attn3, 0, tm=tm_s)
    tall_p, tall_s = (), ()
    for l in range(depth):
        lam_init = _lam_init(l)

        qa, ka_w, ka_t, va_w, va_t, qb, kb_w, kb_t, vb_w, vb_t = _proj(xnp, w_in_b, l, *tabs_p, tall_p, tm=tm_p)
        tall_p = (ka_t, va_t, kb_t, vb_t)
        as3 = lambda a: a.reshape(bp, tp, sec)
        oa = _diff_attn(as3(qa), as3(ka_w), as3(va_w), lams, subln3, l, tq=tq, lam_init=lam_init)
        ob = _moba_attn(as3(qb), as3(kb_w), as3(vb_w), tq=tq)
        out = dense_tail(xp, oa.reshape(bp * tp, sec), ob.reshape(bp * tp, sec), l, tm_p)
        xp, xnp = (out, None) if l == depth - 1 else out

        qa, _, ka_t, _, va_t, qb, _, kb_t, _, vb_t = _proj(xns, w_in_b, l, *tabs_s, tall_s, tm=tm_s)
        tall_s = (ka_t, va_t, kb_t, vb_t)
        qa_st = _stack_queries(qa, bs, ts, n_heads)[:, :, None] * comp_mask[None, None, :, None, :]
        qa_st = qa_st.reshape(bs, n_heads * 2 * ROW_PAD, HEAD_DIM)
        qb_st = _stack_queries(qb, bs, ts, n_heads).reshape(bs, n_heads * ROW_PAD, HEAD_DIM)
        new_pages = tuple(_new_page(a[l], bs, ts, n_heads, page) for a in tall_s)
        oa = _dec_diff_attn(page_table, qa_st, new_pages[0], new_pages[1], caches[0], caches[1], lams, subln3,
                            layer=l, lam_init=lam_init, n_new=ts, pps=pps)
        kmean = _dec_kmean(page_table, caches[2], layer=l, n_heads=n_heads, pps=pps)
        sel = _dec_select(qb_st, kmean, n_heads=n_heads)
        ob = _dec_moba_attn(page_table, qb_st, sel, new_pages[2], new_pages[3], caches[2], caches[3],
                            layer=l, n_new=ts, n_heads=n_heads, pps=pps)
        out = dense_tail(xs, _unstack_heads(oa, bs, ts, n_heads), _unstack_heads(ob, bs, ts, n_heads), l, tm_s)
        xs, xns = (out, None) if l == depth - 1 else out

    new_p = tuple(a.reshape(depth, bp, tp, n_heads, HEAD_DIM) for a in tall_p)
    new_s = tuple(a.reshape(depth, bs, ts, n_heads, HEAD_DIM) for a in tall_s)
    return (xp.reshape(bp, tp, d), xs.reshape(bs, ts, d)) + new_p + new_s
```

```python
import functools
import math

import jax
import jax.numpy as jnp
from jax import lax
from jax.experimental import pallas as pl
from jax.experimental.pallas import tpu as pltpu

F32 = jnp.float32
BF16 = jnp.bfloat16

HEAD_DIM = 128
DQK_A = HEAD_DIM // 2
ROT_A = DQK_A // 4
ROT_B = HEAD_DIM // 4
ROPE_THETA = 500000.0
MOBA_BLOCK = 256
MOBA_TOPK = 3
NORM_EPS = 1e-5
NEG = -1e30
N_SECTIONS = 6
V7X_VMEM_BYTES = 64 * 1024 * 1024

TM_DENSE = 512
TM_PROJ = 1024
TF_MLP = 1024
TQ_ATTN = 512
DEC_PAGES_PER_STEP = 4
KMEAN_PAGES_PER_STEP = 8

_NT = (((1,), (1,)), ((), ()))


def _vmem_limit(block_bytes):
    want = 2 * block_bytes + 8 * 1024 * 1024
    return int(min(want, V7X_VMEM_BYTES - 8 * 1024 * 1024))


def _nbytes(shape, dtype):
    return math.prod(shape) * jnp.dtype(dtype).itemsize


def _rmsnorm(x, g):
    return x * lax.rsqrt(jnp.mean(x * x, axis=-1, keepdims=True) + NORM_EPS) * g


def _layer_vec(width, layer):
    return pl.BlockSpec((None, 1, width), lambda *_: (layer, 0, 0))


def _rope_tables(pos, rot, comp_width):
    half = rot // 2
    freqs = jnp.exp(-math.log(ROPE_THETA) * jnp.arange(half, dtype=F32) / half)
    ang = pos.astype(F32)[:, None] * freqs[None, :]
    cos, sin = jnp.cos(ang), jnp.sin(ang)
    n = pos.shape[0]
    rest = comp_width - rot
    zeros_h = jnp.zeros((n, half), F32)
    c = jnp.concatenate([cos, cos, jnp.ones((n, rest), F32)], axis=-1)
    s_next = jnp.concatenate([-sin, zeros_h, jnp.zeros((n, rest), F32)], axis=-1)
    s_prev = jnp.concatenate([zeros_h, sin, jnp.zeros((n, rest), F32)], axis=-1)
    reps = HEAD_DIM // comp_width
    return tuple(jnp.tile(t, (1, reps)) for t in (c, s_next, s_prev))


def _tile_cast_kernel(x_ref, o_ref):
    o_ref[...] = x_ref[...].astype(o_ref.dtype)


def _tile_columns_bf16(w, tn):
    depth, d, n = w.shape
    return pl.pallas_call(
        _tile_cast_kernel,
        grid=(depth, n // tn),
        in_specs=[pl.BlockSpec((None, d, tn), lambda l, j: (l, 0, j))],
        out_specs=pl.BlockSpec((None, None, d, tn), lambda l, j: (l, j, 0, 0)),
        out_shape=jax.ShapeDtypeStruct((depth, n // tn, d, tn), BF16),
        compiler_params=pltpu.CompilerParams(
            dimension_semantics=("arbitrary", "arbitrary"),
            vmem_limit_bytes=_vmem_limit(_nbytes((d, tn), F32) + _nbytes((d, tn), BF16))),
        name="tile_cast",
    )(w)


def _norm_kernel(x_ref, g_ref, o_ref):
    o_ref[...] = _rmsnorm(x_ref[...], g_ref[...]).astype(o_ref.dtype)


def _norm(x, g, layer, *, tm):
    n, d = x.shape
    row = pl.BlockSpec((tm, d), lambda i: (i, 0))
    return pl.pallas_call(
        _norm_kernel,
        grid=(n // tm,),
        in_specs=[row, _layer_vec(d, layer)],
        out_specs=row,
        out_shape=jax.ShapeDtypeStruct((n, d), BF16),
        compiler_params=pltpu.CompilerParams(
            dimension_semantics=("arbitrary",),
            vmem_limit_bytes=_vmem_limit(_nbytes((tm, d), F32) + _nbytes((tm, d), BF16))),
        name="norm",
    )(x, g)


def _store_heads(y, row0, tabs, half, scale, wide_ref, tall_ref):
    rows = y.shape[0]
    n_heads = y.shape[1] // HEAD_DIM
    if tabs is not None:
        c, s_next, s_prev = (t[row0:row0 + rows, :] for t in tabs)
    for h in range(n_heads):
        sl = slice(h * HEAD_DIM, (h + 1) * HEAD_DIM)
        r = y[:, sl]
        if tabs is not None:
            r = (r * c + pltpu.roll(r, HEAD_DIM - half, 1) * s_next + pltpu.roll(r, half, 1) * s_prev)
        if tall_ref is not None:
            tall_ref[pl.ds(row0 * n_heads + h, rows, stride=n_heads), :] = r
        if wide_ref is not None:
            wide_ref[row0:row0 + rows, sl] = (r * scale if scale != 1.0 else r).astype(wide_ref.dtype)


def _proj_section_kernel(*refs, n_tabs, n_prev, half, scale, has_tall, n_split):
    xn_ref, w_ref = refs[:2]
    tabs = refs[2:2 + n_tabs] if n_tabs else None
    outs = refs[2 + n_tabs + n_prev:]
    wide_ref, tall_ref = outs[0], (outs[1] if has_tall else None)
    rows = xn_ref.shape[0] // n_split
    for s in range(n_split):
        y = jnp.dot(xn_ref[s * rows:(s + 1) * rows, :], w_ref[...], preferred_element_type=F32)
        _store_heads(y, s * rows, tabs, half, scale, wide_ref, tall_ref)


def _proj_section(xn, w, layer, section, tabs, prev_tall, *, tm, half, scale, wide_dtype, has_tall):
    n, d = xn.shape
    depth, _, _, sec = w.shape
    n_heads = sec // HEAD_DIM
    prev = () if prev_tall is None else (prev_tall,)
    in_specs = [pl.BlockSpec((tm, d), lambda i: (i, 0)),
                pl.BlockSpec((None, None, d, sec), lambda i: (layer, section, 0, 0))]
    if tabs:
        n_tab = tabs[0].shape[0] // tm
        in_specs += [pl.BlockSpec((tm, HEAD_DIM), lambda i: (i % n_tab, 0))] * len(tabs)
    in_specs += [pl.BlockSpec(memory_space=pl.ANY)] * len(prev)
    out_specs = [pl.BlockSpec((tm, sec), lambda i: (i, 0))]
    out_shape = [jax.ShapeDtypeStruct((n, sec), wide_dtype)]
    if has_tall:
        out_specs.append(pl.BlockSpec((None, tm * n_heads, HEAD_DIM), lambda i: (layer, i, 0)))
        out_shape.append(jax.ShapeDtypeStruct((depth, n * n_heads, HEAD_DIM), F32))
    block_bytes = (_nbytes((tm, d), BF16) + _nbytes((d, sec), BF16) + len(tabs) * _nbytes((tm, HEAD_DIM), F32)
                   + _nbytes((tm, sec), wide_dtype) + has_tall * _nbytes((tm, sec), F32))
    n_split = 2 if tm % 32 == 0 and tm >= 512 else 1
    return pl.pallas_call(
        functools.partial(_proj_section_kernel, n_tabs=len(tabs), n_prev=len(prev), half=half, scale=scale,
                          has_tall=has_tall, n_split=n_split),
        grid=(n // tm,),
        in_specs=in_specs,
        out_specs=out_specs,
        out_shape=out_shape,
        input_output_aliases={2 + len(tabs): 1} if prev else {},
        compiler_params=pltpu.CompilerParams(
            dimension_semantics=("arbitrary",), vmem_limit_bytes=_vmem_limit(block_bytes)),
        name="proj",
    )(xn, w, *tabs, *prev)


def _proj(xn, w, layer, tabs_a, tabs_b, prev_tall, *, tm):
    prev = iter(prev_tall if prev_tall else (None,) * 4)
    sec = functools.partial(_proj_section, xn, w, layer, tm=tm)
    (qa,) = sec(0, tabs_a, None, half=ROT_A // 2, scale=DQK_A ** -0.5, wide_dtype=BF16, has_tall=False)
    ka_w, ka_t = sec(1, tabs_a, next(prev), half=ROT_A // 2, scale=1.0, wide_dtype=BF16, has_tall=True)
    va_w, va_t = sec(2, (), next(prev), half=0, scale=1.0, wide_dtype=BF16, has_tall=True)
    (qb,) = sec(3, tabs_b, None, half=ROT_B // 2, scale=1.0, wide_dtype=F32, has_tall=False)
    kb_w, kb_t = sec(4, tabs_b, next(prev), half=ROT_B // 2, scale=1.0, wide_dtype=BF16, has_tall=True)
    vb_w, vb_t = sec(5, (), next(prev), half=0, scale=1.0, wide_dtype=BF16, has_tall=True)
    return qa, ka_w, ka_t, va_w, va_t, qb, kb_w, kb_t, vb_w, vb_t


def _lambda(lq1, lk1, lq2, lk2, lam_init):
    a = jnp.exp(jnp.sum(lq1[...] * lk1[...], axis=-1, keepdims=True))
    b = jnp.exp(jnp.sum(lq2[...] * lk2[...], axis=-1, keepdims=True))
    return a - b + lam_init


def _split_components(q):
    lane = lax.broadcasted_iota(jnp.int32, q.shape, 1)
    zero = jnp.zeros_like(q)
    return jnp.where(lane < DQK_A, q, zero), jnp.where(lane >= DQK_A, q, zero)


def _scores(q, k):
    return lax.dot_general(q, k, _NT, preferred_element_type=F32)


def _softmax_first(s, v):
    m = jnp.max(s, axis=-1, keepdims=True)
    p = jnp.exp(s - m)
    return m, jnp.sum(p, axis=-1, keepdims=True), jnp.dot(p.astype(BF16), v, preferred_element_type=F32)


def _softmax_update(state, s, v):
    m, l, acc = state
    m_new = jnp.maximum(m, jnp.max(s, axis=-1, keepdims=True))
    alpha = jnp.exp(m - m_new)
    p = jnp.exp(s - m_new)
    return (m_new, alpha * l + jnp.sum(p, axis=-1, keepdims=True),
            alpha * acc + jnp.dot(p.astype(BF16), v, preferred_element_type=F32))


def _subln(o, g, lam_init):
    return _rmsnorm(o, g) * (1.0 - lam_init)


def _causal(tq):
    return (lax.broadcasted_iota(jnp.int32, (tq, tq), 1) <= lax.broadcasted_iota(jnp.int32, (tq, tq), 0))


def _diff_attn_kernel(lq1, lk1, lq2, lk2, g_ref, q_ref, k_ref, v_ref, o_ref, *, tq, lam_init):
    qi = pl.program_id(2)
    q1, q2 = _split_components(q_ref[...])

    def kv(j):
        start = pl.multiple_of(j * tq, tq)
        return k_ref[pl.ds(start, tq), :], v_ref[pl.ds(start, tq), :]

    k, v = kv(qi)
    causal = _causal(tq)
    st1 = _softmax_first(jnp.where(causal, _scores(q1, k), NEG), v)
    st2 = _softmax_first(jnp.where(causal, _scores(q2, k), NEG), v)

    def body(j, carry):
        st1, st2 = carry
        k, v = kv(j)
        return _softmax_update(st1, _scores(q1, k), v), _softmax_update(st2, _scores(q2, k), v)

    (_, l1, a1), (_, l2, a2) = lax.fori_loop(0, qi, body, (st1, st2))
    lam = _lambda(lq1, lk1, lq2, lk2, lam_init)
    o = a1 / l1 - lam * (a2 / l2)
    o_ref[...] = _subln(o, g_ref[...], lam_init).astype(o_ref.dtype)


def _diff_attn(q, k, v, lams, g, layer, *, tq, lam_init):
    b, t, w = q.shape
    kv_spec = pl.BlockSpec((None, t, HEAD_DIM), lambda bi, h, qi: (bi, 0, h))
    q_spec = pl.BlockSpec((None, tq, HEAD_DIM), lambda bi, h, qi: (bi, qi, h))
    block_bytes = 2 * _nbytes((t, HEAD_DIM), BF16) + 2 * _nbytes((tq, HEAD_DIM), BF16) + 4 * _nbytes((tq, tq), F32)
    return pl.pallas_call(
        functools.partial(_diff_attn_kernel, tq=tq, lam_init=lam_init),
        grid=(b, w // HEAD_DIM, t // tq),
        in_specs=[_layer_vec(DQK_A, layer)] * 4 + [_layer_vec(HEAD_DIM, layer), q_spec, kv_spec, kv_spec],
        out_specs=q_spec,
        out_shape=jax.ShapeDtypeStruct((b, t, w), BF16),
        compiler_params=pltpu.CompilerParams(
            dimension_semantics=("arbitrary",) * 3, vmem_limit_bytes=_vmem_limit(block_bytes)),
        name="diff_attn",
    )(*lams, g, q, k, v)


def _allowed_blocks(gate_t, own, nb):
    blk = lax.broadcasted_iota(jnp.int32, gate_t.shape, 0)
    past = blk < own
    gate_t = jnp.where(past, gate_t, -jnp.inf)
    rank = jnp.zeros(gate_t.shape, F32)
    for m in range(nb):
        gm = gate_t[m:m + 1, :]
        ahead = (gm > gate_t) | ((gm == gate_t) & (m < blk))
        rank = rank + jnp.where(ahead, 1.0, 0.0)
    return jnp.where((past & (rank < MOBA_TOPK)) | (blk == own), 1.0, 0.0)


def _moba_kernel(q_ref, k_ref, v_ref, o_ref, kmean_ref, *, nb, tq):
    qi = pl.program_id(2)
    bpt = tq // MOBA_BLOCK

    @pl.when(qi == 0)
    def _():
        k = k_ref[...].astype(F32)
        kmean_ref[...] = jnp.zeros(kmean_ref.shape, F32)
        kmean_ref[:nb, :] = jnp.mean(k.reshape(nb, MOBA_BLOCK, HEAD_DIM), axis=1)

    q = q_ref[...]
    gate_t = lax.dot_general(kmean_ref[...], q, _NT, precision=lax.Precision.HIGHEST,
                             preferred_element_type=F32)
    own = qi * bpt + lax.broadcasted_iota(jnp.int32, gate_t.shape, 1) // MOBA_BLOCK
    allowed_t = _allowed_blocks(gate_t, own, nb).astype(BF16)
    allowed_t = jnp.concatenate(
        [allowed_t, jnp.zeros((HEAD_DIM - allowed_t.shape[0], tq), BF16)], axis=0)
    eye = (lax.broadcasted_iota(jnp.int32, (tq, tq), 0)
           == lax.broadcasted_iota(jnp.int32, (tq, tq), 1)).astype(BF16)
    allowed = lax.dot_general(eye, allowed_t, _NT, preferred_element_type=F32)
    blk_lane = lax.broadcasted_iota(jnp.int32, allowed.shape, 1)

    qs = (q * HEAD_DIM ** -0.5).astype(BF16)

    def kv(j):
        start = pl.multiple_of(j * tq, tq)
        return k_ref[pl.ds(start, tq), :], v_ref[pl.ds(start, tq), :]

    def block_scores(j, k):
        s = _scores(qs, k)
        parts = []
        for i in range(bpt):
            ok = jnp.sum(jnp.where(blk_lane == j * bpt + i, allowed, 0.0), axis=-1, keepdims=True)
            parts.append(s[:, i * MOBA_BLOCK:(i + 1) * MOBA_BLOCK] + (1.0 - ok) * NEG)
        return jnp.concatenate(parts, axis=1)

    k, v = kv(qi)
    state = _softmax_first(jnp.where(_causal(tq), block_scores(qi, k), NEG), v)

    def body(j, state):
        k, v = kv(j)
        return _softmax_update(state, block_scores(j, k), v)

    _, l, acc = lax.fori_loop(0, qi, body, state)
    o_ref[...] = (acc / l).astype(o_ref.dtype)


def _moba_attn(q, k, v, *, tq):
    b, t, w = q.shape
    nb = t // MOBA_BLOCK
    nb_pad = -(-nb // 8) * 8
    kv_spec = pl.BlockSpec((None, t, HEAD_DIM), lambda bi, h, qi: (bi, 0, h))
    q_spec = pl.BlockSpec((None, tq, HEAD_DIM), lambda bi, h, qi: (bi, qi, h))
    block_bytes = 2 * _nbytes((t, HEAD_DIM), BF16) + 2 * _nbytes((tq, HEAD_DIM), F32) + 4 * _nbytes((tq, tq), F32)
    return pl.pallas_call(
        functools.partial(_moba_kernel, nb=nb, tq=tq),
        grid=(b, w // HEAD_DIM, t // tq),
        in_specs=[q_spec, kv_spec, kv_spec],
        out_specs=q_spec,
        out_shape=jax.ShapeDtypeStruct((b, t, w), BF16),
        scratch_shapes=[pltpu.VMEM((nb_pad, HEAD_DIM), F32)],
        compiler_params=pltpu.CompilerParams(
            dimension_semantics=("arbitrary",) * 3, vmem_limit_bytes=_vmem_limit(block_bytes)),
        name="moba_attn",
    )(q, k, v)


def _out_proj_kernel(x_ref, oa_ref, ob_ref, w_ref, o_ref):
    wa = oa_ref.shape[1]
    o_ref[...] = (x_ref[...]
                  + jnp.dot(oa_ref[...], w_ref[:wa, :], preferred_element_type=F32)
                  + jnp.dot(ob_ref[...], w_ref[wa:, :], preferred_element_type=F32))


def _out_proj(x, oa, ob, w, layer, *, tm):
    n, d = x.shape
    wa, wb = oa.shape[1], ob.shape[1]
    row = lambda width: pl.BlockSpec((tm, width), lambda i: (i, 0))
    block_bytes = (2 * _nbytes((tm, d), F32) + _nbytes((tm, wa + wb), BF16) + _nbytes(w.shape[1:], BF16))
    return pl.pallas_call(
        _out_proj_kernel,
        grid=(n // tm,),
        in_specs=[row(d), row(wa), row(wb), pl.BlockSpec((None,) + w.shape[1:], lambda i: (layer, 0, 0))],
        out_specs=row(d),
        out_shape=jax.ShapeDtypeStruct((n, d), F32),
        compiler_params=pltpu.CompilerParams(
            dimension_semantics=("arbitrary",), vmem_limit_bytes=_vmem_limit(block_bytes)),
        name="out_proj",
    )(x, oa, ob, w)


def _mlp_kernel(x_ref, g_ref, wu_ref, wd_ref, gn_ref, o_ref, *scratch_or_out, last_layer):
    if last_layer:
        (xn_ref,) = scratch_or_out
    else:
        xn_out_ref, xn_ref = scratch_or_out
    f = pl.program_id(1)

    @pl.when(f == 0)
    def _():
        x = x_ref[...]
        xn_ref[...] = _rmsnorm(x, g_ref[...]).astype(BF16)
        o_ref[...] = x

    h = jnp.dot(xn_ref[...], wu_ref[...], preferred_element_type=F32)
    h = jnp.square(jnp.maximum(h, 0.0)).astype(BF16)
    o_ref[...] += jnp.dot(h, wd_ref[...], preferred_element_type=F32)

    @pl.when(f == pl.num_programs(1) - 1)
    def _():
        y = _rmsnorm(o_ref[...], gn_ref[...])
        if last_layer:
            o_ref[...] = y
        else:
            xn_out_ref[...] = y.astype(BF16)


def _mlp(x, g, wu, wd, g_next, layer, next_row, *, tm, last_layer):
    n, d = x.shape
    _, n_f, _, tf = wu.shape
    row = pl.BlockSpec((tm, d), lambda i, f: (i, 0))
    block_bytes = (2 * _nbytes((tm, d), F32) + 2 * _nbytes((d, tf), BF16) + 2 * _nbytes((tm, d), BF16)
                   + _nbytes((tm, tf), F32))
    f32_out = jax.ShapeDtypeStruct((n, d), F32)
    return pl.pallas_call(
        functools.partial(_mlp_kernel, last_layer=last_layer),
        grid=(n // tm, n_f),
        in_specs=[row, _layer_vec(d, layer),
                  pl.BlockSpec((None, None, d, tf), lambda i, f: (layer, f, 0, 0)),
                  pl.BlockSpec((None, tf, d), lambda i, f: (layer, f, 0)),
                  _layer_vec(d, next_row)],
        out_specs=row if last_layer else [row, row],
        out_shape=f32_out if last_layer else [f32_out, jax.ShapeDtypeStruct((n, d), BF16)],
        scratch_shapes=[pltpu.VMEM((tm, d), BF16)],
        compiler_params=pltpu.CompilerParams(
            dimension_semantics=("arbitrary", "arbitrary"), vmem_limit_bytes=_vmem_limit(block_bytes)),
        name="mlp",
    )(x, g, wu, wd, g_next)


def _head_bias(q_rows, t_pad, cols, n_heads):
    r = lax.broadcasted_iota(jnp.int32, (q_rows, cols), 0)
    c = lax.broadcasted_iota(jnp.int32, (q_rows, cols), 1)
    return jnp.where(c % n_heads == (r // t_pad) % n_heads, 0.0, NEG)


def _new_key_ok(q_rows, cols, n_heads, n_new, t_pad):
    tok_q = lax.broadcasted_iota(jnp.int32, (q_rows, cols), 0) % t_pad
    tok_k = lax.broadcasted_iota(jnp.int32, (q_rows, cols), 1) // n_heads
    return (tok_k <= tok_q) & (tok_k < n_new)


def _page_attn_step(q, kn_ref, vn_ref, k_refs, v_refs, bias_ref, state_refs, *, t_pad, n_heads, n_new, penalty):
    p = pl.program_id(1)
    q_rows = q.shape[0]
    page_rows = kn_ref.shape[0]

    @pl.when(p == 0)
    def _():
        bias = _head_bias(q_rows, t_pad, page_rows, n_heads)
        bias_ref[...] = bias
        s = _scores(q, kn_ref[...].astype(BF16)) + bias
        s = jnp.where(_new_key_ok(q_rows, page_rows, n_heads, n_new, t_pad), s, NEG)
        for r, val in zip(state_refs, _softmax_first(s, vn_ref[...].astype(BF16))):
            r[...] = val

    bias = bias_ref[...]
    parts = []
    for i, k in enumerate(k_refs):
        part = _scores(q, k[...].astype(BF16)) + bias
        parts.append(part if penalty is None else part + penalty(i))
    s = jnp.concatenate(parts, axis=1)
    v = jnp.concatenate([v[...].astype(BF16) for v in v_refs], axis=0)
    state = _softmax_update(tuple(r[...] for r in state_refs), s, v)
    for r, val in zip(state_refs, state):
        r[...] = val


def _dec_diff_kernel(pt_ref, lq1, lk1, lq2, lk2, g_ref, q_ref, kn_ref, vn_ref, *refs,
                     lam_init, n_new, n_heads, t_pad, pps):
    del pt_ref
    k_refs, v_refs, o_ref = refs[:pps], refs[pps:2 * pps], refs[2 * pps]
    bias_ref, m_ref, l_ref, acc_ref = refs[2 * pps + 1:]
    _page_attn_step(q_ref[...].astype(BF16), kn_ref, vn_ref, k_refs, v_refs, bias_ref, (m_ref, l_ref, acc_ref),
                    t_pad=t_pad, n_heads=n_heads, n_new=n_new, penalty=None)

    @pl.when(pl.program_id(1) == pl.num_programs(1) - 1)
    def _():
        lam = _lambda(lq1, lk1, lq2, lk2, lam_init)
        o = acc_ref[...] / l_ref[...]
        half = o.shape[0] // 2
        o_ref[...] = _subln(o[:half] - lam * o[half:], g_ref[...], lam_init)


def _paged_specs(page_table, cache, layer, pps):
    page_rows = cache.shape[2]

    def spec(i):
        return pl.BlockSpec((None, None, page_rows, HEAD_DIM),
                            lambda bi, p, pt: (layer, pt[bi, p * pps + i], 0, 0))
    return [spec(i) for i in range(pps)]


def _dec_diff_attn(page_table, q, k_new, v_new, cache_k, cache_v, lams, g, *,
                   layer, lam_init, n_new, n_heads, t_pad, pps):
    b, q_rows, _ = q.shape
    n_pages = page_table.shape[1]
    page_rows = cache_k.shape[2]
    vec = lambda width: pl.BlockSpec((None, 1, width), lambda bi, p, pt: (layer, 0, 0))
    per_b = lambda r: pl.BlockSpec((None, r, HEAD_DIM), lambda bi, p, pt: (bi, 0, 0))
    block_bytes = (2 + 2 * pps) * _nbytes((page_rows, HEAD_DIM), F32) + 4 * pps * _nbytes((q_rows, page_rows), F32)
    return pl.pallas_call(
        functools.partial(_dec_diff_kernel, lam_init=lam_init, n_new=n_new, n_heads=n_heads, t_pad=t_pad, pps=pps),
        grid_spec=pltpu.PrefetchScalarGridSpec(
            num_scalar_prefetch=1,
            grid=(b, n_pages // pps),
            in_specs=[vec(DQK_A)] * 4 + [vec(HEAD_DIM), per_b(q_rows), per_b(page_rows), per_b(page_rows)]
                     + _paged_specs(page_table, cache_k, layer, pps) + _paged_specs(page_table, cache_v, layer, pps),
            out_specs=per_b(q_rows // 2),
            scratch_shapes=[pltpu.VMEM((q_rows, page_rows), F32), pltpu.VMEM((q_rows, 1), F32),
                            pltpu.VMEM((q_rows, 1), F32), pltpu.VMEM((q_rows, HEAD_DIM), F32)]),
        out_shape=jax.ShapeDtypeStruct((b, q_rows // 2, HEAD_DIM), F32),
        compiler_params=pltpu.CompilerParams(
            dimension_semantics=("arbitrary", "arbitrary"), vmem_limit_bytes=_vmem_limit(block_bytes)),
        name="dec_diff_attn",
    )(page_table, *lams, g, q, k_new, v_new, *([cache_k] * pps), *([cache_v] * pps))


def _dec_kmean_kernel(pt_ref, *refs, n_heads, ppb):
    del pt_ref
    o_ref = refs[-1]
    sums = [jnp.sum(r[...].reshape(-1, n_heads, HEAD_DIM), axis=0) for r in refs[:-1]]
    for i in range(len(sums) // ppb):
        o_ref[i * n_heads:(i + 1) * n_heads, :] = sum(sums[i * ppb:(i + 1) * ppb]) * (1.0 / MOBA_BLOCK)


def _dec_kmean(page_table, cache_k, *, layer, n_heads, pps):
    b, n_pages = page_table.shape
    page_rows = cache_k.shape[2]
    ppb = MOBA_BLOCK // (page_rows // n_heads)
    nblk = n_pages // ppb
    bps = pps // ppb
    return pl.pallas_call(
        functools.partial(_dec_kmean_kernel, n_heads=n_heads, ppb=ppb),
        grid_spec=pltpu.PrefetchScalarGridSpec(
            num_scalar_prefetch=1,
            grid=(b, n_pages // pps),
            in_specs=_paged_specs(page_table, cache_k, layer, pps),
            out_specs=pl.BlockSpec((None, bps * n_heads, HEAD_DIM), lambda bi, n, pt: (bi, n, 0))),
        out_shape=jax.ShapeDtypeStruct((b, nblk * n_heads, HEAD_DIM), F32),
        compiler_params=pltpu.CompilerParams(
            dimension_semantics=("arbitrary", "arbitrary"),
            vmem_limit_bytes=_vmem_limit(pps * _nbytes((page_rows, HEAD_DIM), F32))),
        name="dec_kmean",
    )(page_table, *([cache_k] * pps))


def _dec_select_kernel(q_ref, kmean_ref, o_ref, *, n_heads, t_pad):
    gate = lax.dot_general(q_ref[...], kmean_ref[...], _NT, precision=lax.Precision.HIGHEST,
                           preferred_element_type=F32)
    r = lax.broadcasted_iota(jnp.int32, gate.shape, 0)
    c = lax.broadcasted_iota(jnp.int32, gate.shape, 1)
    gate = jnp.where(c % n_heads == r // t_pad, gate, -jnp.inf)
    col = c.astype(F32)
    chosen = jnp.zeros(gate.shape, F32)
    for _ in range(MOBA_TOPK):
        best = jnp.max(gate, axis=-1, keepdims=True)
        pick = jnp.min(jnp.where(gate == best, col, float(gate.shape[1])), axis=-1, keepdims=True)
        chosen = jnp.where(col == pick, 1.0, chosen)
        gate = jnp.where(col == pick, -jnp.inf, gate)
    o_ref[...] = chosen


def _dec_select(q, kmean, *, n_heads, t_pad):
    b, q_rows, _ = q.shape
    km_rows = kmean.shape[1]
    return pl.pallas_call(
        functools.partial(_dec_select_kernel, n_heads=n_heads, t_pad=t_pad),
        grid=(b,),
        in_specs=[pl.BlockSpec((None, q_rows, HEAD_DIM), lambda bi: (bi, 0, 0)),
                  pl.BlockSpec((None, km_rows, HEAD_DIM), lambda bi: (bi, 0, 0))],
        out_specs=pl.BlockSpec((None, q_rows, km_rows), lambda bi: (bi, 0, 0)),
        out_shape=jax.ShapeDtypeStruct((b, q_rows, km_rows), F32),
        compiler_params=pltpu.CompilerParams(dimension_semantics=("arbitrary",)),
        name="dec_select",
    )(q, kmean)


def _dec_moba_kernel(pt_ref, q_ref, sel_ref, kn_ref, vn_ref, *refs, n_new, n_heads, t_pad, pps, ppb):
    del pt_ref
    k_refs, v_refs, o_ref = refs[:pps], refs[pps:2 * pps], refs[2 * pps]
    bias_ref, m_ref, l_ref, acc_ref = refs[2 * pps + 1:]
    sel = sel_ref[...]
    blk_lane = lax.broadcasted_iota(jnp.int32, sel.shape, 1) // n_heads
    first_blk = pl.program_id(1) * (pps // ppb)
    penalties = []
    for i in range(pps // ppb):
        chosen = jnp.sum(jnp.where(blk_lane == first_blk + i, sel, 0.0), axis=-1, keepdims=True)
        penalties.append((1.0 - chosen) * NEG)
    q = (q_ref[...] * HEAD_DIM ** -0.5).astype(BF16)
    _page_attn_step(q, kn_ref, vn_ref, k_refs, v_refs, bias_ref, (m_ref, l_ref, acc_ref),
                    t_pad=t_pad, n_heads=n_heads, n_new=n_new, penalty=lambda i: penalties[i // ppb])

    @pl.when(pl.program_id(1) == pl.num_programs(1) - 1)
    def _():
        o_ref[...] = acc_ref[...] / l_ref[...]


def _dec_moba_attn(page_table, q, sel, k_new, v_new, cache_k, cache_v, *, layer, n_new, n_heads, t_pad, pps):
    b, q_rows, _ = q.shape
    n_pages = page_table.shape[1]
    page_rows = cache_k.shape[2]
    ppb = MOBA_BLOCK // (page_rows // n_heads)
    per_b = lambda r, c: pl.BlockSpec((None, r, c), lambda bi, p, pt: (bi, 0, 0))
    block_bytes = (2 + 2 * pps) * _nbytes((page_rows, HEAD_DIM), F32) + 4 * pps * _nbytes((q_rows, page_rows), F32)
    return pl.pallas_call(
        functools.partial(_dec_moba_kernel, n_new=n_new, n_heads=n_heads, t_pad=t_pad, pps=pps, ppb=ppb),
        grid_spec=pltpu.PrefetchScalarGridSpec(
            num_scalar_prefetch=1,
            grid=(b, n_pages // pps),
            in_specs=[per_b(q_rows, HEAD_DIM), per_b(q_rows, sel.shape[2]),
                      per_b(page_rows, HEAD_DIM), per_b(page_rows, HEAD_DIM)]
                     + _paged_specs(page_table, cache_k, layer, pps) + _paged_specs(page_table, cache_v, layer, pps),
            out_specs=per_b(q_rows, HEAD_DIM),
            scratch_shapes=[pltpu.VMEM((q_rows, page_rows), F32), pltpu.VMEM((q_rows, 1), F32),
                            pltpu.VMEM((q_rows, 1), F32), pltpu.VMEM((q_rows, HEAD_DIM), F32)]),
        out_shape=jax.ShapeDtypeStruct((b, q_rows, HEAD_DIM), F32),
        compiler_params=pltpu.CompilerParams(
            dimension_semantics=("arbitrary", "arbitrary"), vmem_limit_bytes=_vmem_limit(block_bytes)),
        name="dec_moba_attn",
    )(page_table, q, sel, k_new, v_new, *([cache_k] * pps), *([cache_v] * pps))


def _lam_init(layer):
    return 0.8 - 0.6 * math.exp(-0.3 * layer)


def _stack_queries(q, b, t, n_heads, t_pad):
    q = q.astype(F32).reshape(b, t, n_heads, HEAD_DIM).transpose(0, 2, 1, 3)
    return jnp.pad(q, ((0, 0), (0, 0), (0, t_pad - t), (0, 0)))


def _unstack_heads(o, b, t, n_heads, t_pad):
    o = o.reshape(b, n_heads, t_pad, HEAD_DIM)[:, :, :t].transpose(0, 2, 1, 3)
    return o.reshape(b * t, n_heads * HEAD_DIM).astype(BF16)


def _new_page(tall, b, t, n_heads, page):
    a = tall.reshape(b, t * n_heads, HEAD_DIM)
    return jnp.pad(a, ((0, 0), (0, (page - t) * n_heads), (0, 0)))


def kernel(x_prompt, x_sample, cache_a_k, cache_a_v, cache_b_k, cache_b_v, page_table, norm_attn, w_in, lambda_q1, lambda_k1, lambda_q2, lambda_k2, subln_a, w_out, norm_mlp, w_up, w_down, norm_final):
    depth, d, d_in = w_in.shape
    dff = w_up.shape[2]
    bp, tp, _ = x_prompt.shape
    bs, ts, _ = x_sample.shape
    n_pool, page = cache_a_k.shape[1], cache_a_k.shape[2]
    n_pages = page_table.shape[1]
    past_len = n_pages * page
    n_heads = cache_a_k.shape[3]
    sec = n_heads * HEAD_DIM
    ppb = MOBA_BLOCK // page
    assert cache_b_k.shape[3] == n_heads and d_in == N_SECTIONS * sec
    assert ts <= 8 and past_len % MOBA_BLOCK == 0 and past_len // MOBA_BLOCK >= MOBA_TOPK
    t_pad = 4 if ts <= 4 else 8
    pps = DEC_PAGES_PER_STEP
    assert MOBA_BLOCK % page == 0 and n_pages % pps == 0 and pps % ppb == 0
    assert n_pages % KMEAN_PAGES_PER_STEP == 0 and KMEAN_PAGES_PER_STEP % ppb == 0 and (n_heads * t_pad) % 8 == 0
    tm_p = min(TM_DENSE, bp * tp)
    tm_proj = TM_PROJ if tp % TM_PROJ == 0 else tm_p
    tq = min(TQ_ATTN, tp)
    tf = min(TF_MLP, dff)
    assert tp % tq == 0 and tq % MOBA_BLOCK == 0 and (bp * tp) % tm_p == 0 and tp % tm_p == 0 and dff % tf == 0

    w_in_b = _tile_columns_bf16(w_in, sec)
    w_up_b = _tile_columns_bf16(w_up, tf)
    w_out_b, w_down_b = w_out.astype(BF16), w_down.astype(BF16)
    caches = tuple(c.reshape(depth, n_pool, page * n_heads, HEAD_DIM)
                   for c in (cache_a_k, cache_a_v, cache_b_k, cache_b_v))
    norm_attn3, norm_mlp3, subln3 = norm_attn[:, None], norm_mlp[:, None], subln_a[:, None]
    norm_final3 = norm_final[None, None]
    lams = tuple(v[:, None] for v in (lambda_q1, lambda_k1, lambda_q2, lambda_k2))

    pos_p = jnp.arange(tp)
    pos_s = jnp.tile(past_len + jnp.arange(ts), bs)
    tabs_p = (_rope_tables(pos_p, ROT_A, DQK_A), _rope_tables(pos_p, ROT_B, HEAD_DIM))
    tabs_s = (_rope_tables(pos_s, ROT_A, DQK_A), _rope_tables(pos_s, ROT_B, HEAD_DIM))
    comp_mask = (jnp.arange(HEAD_DIM)[None, :] // DQK_A == jnp.arange(2)[:, None]).astype(F32)

    tm_s = bs * ts

    def dense_tail(x, oa, ob, l, tm):
        x = _out_proj(x, oa, ob, w_out_b, l, tm=tm)
        last = l == depth - 1
        g_next, next_row = (norm_final3, 0) if last else (norm_attn3, l + 1)
        return _mlp(x, norm_mlp3, w_up_b, w_down_b, g_next, l, next_row, tm=tm, last_layer=last)

    xp = x_prompt.reshape(bp * tp, d)
    xs = x_sample.reshape(bs * ts, d)
    xnp = _norm(xp, norm_attn3, 0, tm=tm_p)
    xns = _norm(xs, norm_attn3, 0, tm=tm_s)
    tall_p, tall_s = (), ()
    for l in range(depth):
        lam_init = _lam_init(l)

        qa, ka_w, ka_t, va_w, va_t, qb, kb_w, kb_t, vb_w, vb_t = _proj(xnp, w_in_b, l, *tabs_p, tall_p, tm=tm_proj)
        tall_p = (ka_t, va_t, kb_t, vb_t)
        as3 = lambda a: a.reshape(bp, tp, sec)
        oa = _diff_attn(as3(qa), as3(ka_w), as3(va_w), lams, subln3, l, tq=tq, lam_init=lam_init)
        ob = _moba_attn(as3(qb), as3(kb_w), as3(vb_w), tq=tq)
        out = dense_tail(xp, oa.reshape(bp * tp, sec), ob.reshape(bp * tp, sec), l, tm_p)
        xp, xnp = (out, None) if l == depth - 1 else out

        qa, _, ka_t, _, va_t, qb, _, kb_t, _, vb_t = _proj(xns, w_in_b, l, *tabs_s, tall_s, tm=tm_s)
        tall_s = (ka_t, va_t, kb_t, vb_t)
        qa_st = _stack_queries(qa, bs, ts, n_heads, t_pad)[:, None] * comp_mask[None, :, None, None, :]
        qa_st = qa_st.reshape(bs, 2 * n_heads * t_pad, HEAD_DIM)
        qb_st = _stack_queries(qb, bs, ts, n_heads, t_pad).reshape(bs, n_heads * t_pad, HEAD_DIM)
        new_pages = tuple(_new_page(a[l], bs, ts, n_heads, page) for a in tall_s)
        oa = _dec_diff_attn(page_table, qa_st, new_pages[0], new_pages[1], caches[0], caches[1], lams, subln3,
                            layer=l, lam_init=lam_init, n_new=ts, n_heads=n_heads, t_pad=t_pad, pps=pps)
        kmean = _dec_kmean(page_table, caches[2], layer=l, n_heads=n_heads, pps=KMEAN_PAGES_PER_STEP)
        sel = _dec_select(qb_st, kmean, n_heads=n_heads, t_pad=t_pad)
        ob = _dec_moba_attn(page_table, qb_st, sel, new_pages[2], new_pages[3], caches[2], caches[3],
                            layer=l, n_new=ts, n_heads=n_heads, t_pad=t_pad, pps=pps)
        out = dense_tail(xs, _unstack_heads(oa, bs, ts, n_heads, t_pad), _unstack_heads(ob, bs, ts, n_heads, t_pad),
                         l, tm_s)
        xs, xns = (out, None) if l == depth - 1 else out

    new_p = tuple(a.reshape(depth, bp, tp, n_heads, HEAD_DIM) for a in tall_p)
    new_s = tuple(a.reshape(depth, bs, ts, n_heads, HEAD_DIM) for a in tall_s)
    return (xp.reshape(bp, tp, d), xs.reshape(bs, ts, d)) + new_p + new_s
```

```python
import functools
import math

import jax
import jax.numpy as jnp
from jax import lax
from jax.experimental import pallas as pl
from jax.experimental.pallas import tpu as pltpu

F32 = jnp.float32
BF16 = jnp.bfloat16

HEAD_DIM = 128
DQK_A = HEAD_DIM // 2
ROT_A = DQK_A // 4
ROT_B = HEAD_DIM // 4
ROPE_THETA = 500000.0
MOBA_BLOCK = 256
MOBA_TOPK = 3
NORM_EPS = 1e-5
NEG = -1e30
N_SECTIONS = 6
V7X_VMEM_BYTES = 64 * 1024 * 1024

TM_DENSE = 512
TM_PROJ = 1024
TM_MLP = 1024
TF_MLP = 512
TQ_ATTN = 512
DEC_PAGES_PER_STEP = 8
KMEAN_PAGES_PER_STEP = 8

_NT = (((1,), (1,)), ((), ()))


def _vmem_limit(block_bytes):
    want = 2 * block_bytes + 8 * 1024 * 1024
    return int(min(want, V7X_VMEM_BYTES - 8 * 1024 * 1024))


def _nbytes(shape, dtype):
    return math.prod(shape) * jnp.dtype(dtype).itemsize


def _rmsnorm(x, g):
    return x * lax.rsqrt(jnp.mean(x * x, axis=-1, keepdims=True) + NORM_EPS) * g


def _layer_vec(width, layer):
    return pl.BlockSpec((None, 1, width), lambda *_: (layer, 0, 0))


def _rope_tables(pos, rot, comp_width):
    half = rot // 2
    freqs = jnp.exp(-math.log(ROPE_THETA) * jnp.arange(half, dtype=F32) / half)
    ang = pos.astype(F32)[:, None] * freqs[None, :]
    cos, sin = jnp.cos(ang), jnp.sin(ang)
    n = pos.shape[0]
    rest = comp_width - rot
    zeros_h = jnp.zeros((n, half), F32)
    c = jnp.concatenate([cos, cos, jnp.ones((n, rest), F32)], axis=-1)
    s_next = jnp.concatenate([-sin, zeros_h, jnp.zeros((n, rest), F32)], axis=-1)
    s_prev = jnp.concatenate([zeros_h, sin, jnp.zeros((n, rest), F32)], axis=-1)
    reps = HEAD_DIM // comp_width
    return tuple(jnp.tile(t, (1, reps)) for t in (c, s_next, s_prev))


def _tile_cast_kernel(x_ref, o_ref):
    o_ref[...] = x_ref[...].astype(o_ref.dtype)


def _tile_columns_bf16(w, tn):
    depth, d, n = w.shape
    return pl.pallas_call(
        _tile_cast_kernel,
        grid=(depth, n // tn),
        in_specs=[pl.BlockSpec((None, d, tn), lambda l, j: (l, 0, j))],
        out_specs=pl.BlockSpec((None, None, d, tn), lambda l, j: (l, j, 0, 0)),
        out_shape=jax.ShapeDtypeStruct((depth, n // tn, d, tn), BF16),
        compiler_params=pltpu.CompilerParams(
            dimension_semantics=("arbitrary", "arbitrary"),
            vmem_limit_bytes=_vmem_limit(_nbytes((d, tn), F32) + _nbytes((d, tn), BF16))),
        name="tile_cast",
    )(w)


def _norm_kernel(x_ref, g_ref, o_ref):
    o_ref[...] = _rmsnorm(x_ref[...], g_ref[...]).astype(o_ref.dtype)


def _norm(x, g, layer, *, tm):
    n, d = x.shape
    row = pl.BlockSpec((tm, d), lambda i: (i, 0))
    return pl.pallas_call(
        _norm_kernel,
        grid=(n // tm,),
        in_specs=[row, _layer_vec(d, layer)],
        out_specs=row,
        out_shape=jax.ShapeDtypeStruct((n, d), BF16),
        compiler_params=pltpu.CompilerParams(
            dimension_semantics=("arbitrary",),
            vmem_limit_bytes=_vmem_limit(_nbytes((tm, d), F32) + _nbytes((tm, d), BF16))),
        name="norm",
    )(x, g)


def _store_heads(y, row0, tabs, half, scale, wide_ref, tall_ref):
    rows = y.shape[0]
    n_heads = y.shape[1] // HEAD_DIM
    if tabs is not None:
        c, s_next, s_prev = (t[row0:row0 + rows, :] for t in tabs)
    for h in range(n_heads):
        sl = slice(h * HEAD_DIM, (h + 1) * HEAD_DIM)
        r = y[:, sl]
        if tabs is not None:
            r = (r * c + pltpu.roll(r, HEAD_DIM - half, 1) * s_next + pltpu.roll(r, half, 1) * s_prev)
        if tall_ref is not None:
            tall_ref[pl.ds(row0 * n_heads + h, rows, stride=n_heads), :] = r
        if wide_ref is not None:
            wide_ref[row0:row0 + rows, sl] = (r * scale if scale != 1.0 else r).astype(wide_ref.dtype)


def _proj_section_kernel(*refs, n_tabs, n_prev, half, scale, has_tall, n_split):
    xn_ref, w_ref = refs[:2]
    tabs = refs[2:2 + n_tabs] if n_tabs else None
    outs = refs[2 + n_tabs + n_prev:]
    wide_ref, tall_ref = outs[0], (outs[1] if has_tall else None)
    rows = xn_ref.shape[0] // n_split
    for s in range(n_split):
        y = jnp.dot(xn_ref[s * rows:(s + 1) * rows, :], w_ref[...], preferred_element_type=F32)
        _store_heads(y, s * rows, tabs, half, scale, wide_ref, tall_ref)


def _proj_section(xn, w, layer, section, tabs, prev_tall, *, tm, half, scale, wide_dtype, has_tall):
    n, d = xn.shape
    depth, _, _, sec = w.shape
    n_heads = sec // HEAD_DIM
    prev = () if prev_tall is None else (prev_tall,)
    in_specs = [pl.BlockSpec((tm, d), lambda i: (i, 0)),
                pl.BlockSpec((None, None, d, sec), lambda i: (layer, section, 0, 0))]
    if tabs:
        n_tab = tabs[0].shape[0] // tm
        in_specs += [pl.BlockSpec((tm, HEAD_DIM), lambda i: (i % n_tab, 0))] * len(tabs)
    in_specs += [pl.BlockSpec(memory_space=pl.ANY)] * len(prev)
    out_specs = [pl.BlockSpec((tm, sec), lambda i: (i, 0))]
    out_shape = [jax.ShapeDtypeStruct((n, sec), wide_dtype)]
    if has_tall:
        out_specs.append(pl.BlockSpec((None, tm * n_heads, HEAD_DIM), lambda i: (layer, i, 0)))
        out_shape.append(jax.ShapeDtypeStruct((depth, n * n_heads, HEAD_DIM), F32))
    block_bytes = (_nbytes((tm, d), BF16) + _nbytes((d, sec), BF16) + len(tabs) * _nbytes((tm, HEAD_DIM), F32)
                   + _nbytes((tm, sec), wide_dtype) + has_tall * _nbytes((tm, sec), F32))
    n_split = 2 if tm % 32 == 0 and tm >= 512 else 1
    return pl.pallas_call(
        functools.partial(_proj_section_kernel, n_tabs=len(tabs), n_prev=len(prev), half=half, scale=scale,
                          has_tall=has_tall, n_split=n_split),
        grid=(n // tm,),
        in_specs=in_specs,
        out_specs=out_specs,
        out_shape=out_shape,
        input_output_aliases={2 + len(tabs): 1} if prev else {},
        compiler_params=pltpu.CompilerParams(
            dimension_semantics=("arbitrary",), vmem_limit_bytes=_vmem_limit(block_bytes)),
        name="proj",
    )(xn, w, *tabs, *prev)


def _proj(xn, w, layer, tabs_a, tabs_b, prev_tall, *, tm):
    prev = iter(prev_tall if prev_tall else (None,) * 4)
    sec = functools.partial(_proj_section, xn, w, layer, tm=tm)
    (qa,) = sec(0, tabs_a, None, half=ROT_A // 2, scale=DQK_A ** -0.5, wide_dtype=BF16, has_tall=False)
    ka_w, ka_t = sec(1, tabs_a, next(prev), half=ROT_A // 2, scale=1.0, wide_dtype=BF16, has_tall=True)
    va_w, va_t = sec(2, (), next(prev), half=0, scale=1.0, wide_dtype=BF16, has_tall=True)
    (qb,) = sec(3, tabs_b, None, half=ROT_B // 2, scale=1.0, wide_dtype=F32, has_tall=False)
    kb_w, kb_t = sec(4, tabs_b, next(prev), half=ROT_B // 2, scale=1.0, wide_dtype=BF16, has_tall=True)
    vb_w, vb_t = sec(5, (), next(prev), half=0, scale=1.0, wide_dtype=BF16, has_tall=True)
    return qa, ka_w, ka_t, va_w, va_t, qb, kb_w, kb_t, vb_w, vb_t


def _lambda(lq1, lk1, lq2, lk2, lam_init):
    a = jnp.exp(jnp.sum(lq1[...] * lk1[...], axis=-1, keepdims=True))
    b = jnp.exp(jnp.sum(lq2[...] * lk2[...], axis=-1, keepdims=True))
    return a - b + lam_init


def _split_components(q):
    lane = lax.broadcasted_iota(jnp.int32, q.shape, 1)
    zero = jnp.zeros_like(q)
    return jnp.where(lane < DQK_A, q, zero), jnp.where(lane >= DQK_A, q, zero)


def _scores(q, k):
    return lax.dot_general(q, k, _NT, preferred_element_type=F32)


def _softmax_first(s, v):
    m = jnp.max(s, axis=-1, keepdims=True)
    p = jnp.exp(s - m)
    return m, jnp.sum(p, axis=-1, keepdims=True), jnp.dot(p.astype(BF16), v, preferred_element_type=F32)


def _softmax_update(state, s, v):
    m, l, acc = state
    m_new = jnp.maximum(m, jnp.max(s, axis=-1, keepdims=True))
    alpha = jnp.exp(m - m_new)
    p = jnp.exp(s - m_new)
    return (m_new, alpha * l + jnp.sum(p, axis=-1, keepdims=True),
            alpha * acc + jnp.dot(p.astype(BF16), v, preferred_element_type=F32))


def _subln(o, g, lam_init):
    return _rmsnorm(o, g) * (1.0 - lam_init)


def _causal(tq):
    return (lax.broadcasted_iota(jnp.int32, (tq, tq), 1) <= lax.broadcasted_iota(jnp.int32, (tq, tq), 0))


def _diff_attn_kernel(lq1, lk1, lq2, lk2, g_ref, q_ref, k_ref, v_ref, o_ref, *, tq, lam_init):
    qi = pl.program_id(2)
    q1, q2 = _split_components(q_ref[...])

    def kv(j):
        start = pl.multiple_of(j * tq, tq)
        return k_ref[pl.ds(start, tq), :], v_ref[pl.ds(start, tq), :]

    k, v = kv(qi)
    causal = _causal(tq)
    st1 = _softmax_first(jnp.where(causal, _scores(q1, k), NEG), v)
    st2 = _softmax_first(jnp.where(causal, _scores(q2, k), NEG), v)

    def body(j, carry):
        st1, st2 = carry
        k, v = kv(j)
        return _softmax_update(st1, _scores(q1, k), v), _softmax_update(st2, _scores(q2, k), v)

    (_, l1, a1), (_, l2, a2) = lax.fori_loop(0, qi, body, (st1, st2))
    lam = _lambda(lq1, lk1, lq2, lk2, lam_init)
    o = a1 / l1 - lam * (a2 / l2)
    o_ref[...] = _subln(o, g_ref[...], lam_init).astype(o_ref.dtype)


def _diff_attn(q, k, v, lams, g, layer, *, tq, lam_init):
    b, t, w = q.shape
    kv_spec = pl.BlockSpec((None, t, HEAD_DIM), lambda bi, h, qi: (bi, 0, h))
    q_spec = pl.BlockSpec((None, tq, HEAD_DIM), lambda bi, h, qi: (bi, qi, h))
    block_bytes = 2 * _nbytes((t, HEAD_DIM), BF16) + 2 * _nbytes((tq, HEAD_DIM), BF16) + 4 * _nbytes((tq, tq), F32)
    return pl.pallas_call(
        functools.partial(_diff_attn_kernel, tq=tq, lam_init=lam_init),
        grid=(b, w // HEAD_DIM, t // tq),
        in_specs=[_layer_vec(DQK_A, layer)] * 4 + [_layer_vec(HEAD_DIM, layer), q_spec, kv_spec, kv_spec],
        out_specs=q_spec,
        out_shape=jax.ShapeDtypeStruct((b, t, w), BF16),
        compiler_params=pltpu.CompilerParams(
            dimension_semantics=("arbitrary",) * 3, vmem_limit_bytes=_vmem_limit(block_bytes)),
        name="diff_attn",
    )(*lams, g, q, k, v)


def _allowed_blocks(gate_t, own, nb):
    blk = lax.broadcasted_iota(jnp.int32, gate_t.shape, 0)
    past = blk < own
    gate_t = jnp.where(past, gate_t, -jnp.inf)
    rank = jnp.zeros(gate_t.shape, F32)
    for m in range(nb):
        gm = gate_t[m:m + 1, :]
        ahead = (gm > gate_t) | ((gm == gate_t) & (m < blk))
        rank = rank + jnp.where(ahead, 1.0, 0.0)
    return jnp.where((past & (rank < MOBA_TOPK)) | (blk == own), 1.0, 0.0)


def _moba_kernel(q_ref, k_ref, v_ref, o_ref, kmean_ref, *, nb, tq):
    qi = pl.program_id(2)
    bpt = tq // MOBA_BLOCK

    @pl.when(qi == 0)
    def _():
        k = k_ref[...].astype(F32)
        kmean_ref[...] = jnp.zeros(kmean_ref.shape, F32)
        kmean_ref[:nb, :] = jnp.mean(k.reshape(nb, MOBA_BLOCK, HEAD_DIM), axis=1)

    q = q_ref[...]
    gate_t = lax.dot_general(kmean_ref[...], q, _NT, precision=lax.Precision.HIGHEST,
                             preferred_element_type=F32)
    own = qi * bpt + lax.broadcasted_iota(jnp.int32, gate_t.shape, 1) // MOBA_BLOCK
    allowed_t = _allowed_blocks(gate_t, own, nb).astype(BF16)
    allowed_t = jnp.concatenate(
        [allowed_t, jnp.zeros((HEAD_DIM - allowed_t.shape[0], tq), BF16)], axis=0)
    eye = (lax.broadcasted_iota(jnp.int32, (tq, tq), 0)
           == lax.broadcasted_iota(jnp.int32, (tq, tq), 1)).astype(BF16)
    allowed = lax.dot_general(eye, allowed_t, _NT, preferred_element_type=F32)
    blk_lane = lax.broadcasted_iota(jnp.int32, allowed.shape, 1)

    qs = (q * HEAD_DIM ** -0.5).astype(BF16)

    def kv(j):
        start = pl.multiple_of(j * tq, tq)
        return k_ref[pl.ds(start, tq), :], v_ref[pl.ds(start, tq), :]

    def block_scores(j, k):
        s = _scores(qs, k)
        parts = []
        for i in range(bpt):
            ok = jnp.sum(jnp.where(blk_lane == j * bpt + i, allowed, 0.0), axis=-1, keepdims=True)
            parts.append(s[:, i * MOBA_BLOCK:(i + 1) * MOBA_BLOCK] + (1.0 - ok) * NEG)
        return jnp.concatenate(parts, axis=1)

    k, v = kv(qi)
    state = _softmax_first(jnp.where(_causal(tq), block_scores(qi, k), NEG), v)

    def body(j, state):
        k, v = kv(j)
        return _softmax_update(state, block_scores(j, k), v)

    _, l, acc = lax.fori_loop(0, qi, body, state)
    o_ref[...] = (acc / l).astype(o_ref.dtype)


def _moba_attn(q, k, v, *, tq):
    b, t, w = q.shape
    nb = t // MOBA_BLOCK
    nb_pad = -(-nb // 8) * 8
    kv_spec = pl.BlockSpec((None, t, HEAD_DIM), lambda bi, h, qi: (bi, 0, h))
    q_spec = pl.BlockSpec((None, tq, HEAD_DIM), lambda bi, h, qi: (bi, qi, h))
    block_bytes = 2 * _nbytes((t, HEAD_DIM), BF16) + 2 * _nbytes((tq, HEAD_DIM), F32) + 4 * _nbytes((tq, tq), F32)
    return pl.pallas_call(
        functools.partial(_moba_kernel, nb=nb, tq=tq),
        grid=(b, w // HEAD_DIM, t // tq),
        in_specs=[q_spec, kv_spec, kv_spec],
        out_specs=q_spec,
        out_shape=jax.ShapeDtypeStruct((b, t, w), BF16),
        scratch_shapes=[pltpu.VMEM((nb_pad, HEAD_DIM), F32)],
        compiler_params=pltpu.CompilerParams(
            dimension_semantics=("arbitrary",) * 3, vmem_limit_bytes=_vmem_limit(block_bytes)),
        name="moba_attn",
    )(q, k, v)


def _out_proj_kernel(x_ref, oa_ref, ob_ref, w_ref, o_ref):
    wa = oa_ref.shape[1]
    o_ref[...] = (x_ref[...]
                  + jnp.dot(oa_ref[...], w_ref[:wa, :], preferred_element_type=F32)
                  + jnp.dot(ob_ref[...], w_ref[wa:, :], preferred_element_type=F32))


def _out_proj(x, oa, ob, w, layer, *, tm):
    n, d = x.shape
    wa, wb = oa.shape[1], ob.shape[1]
    row = lambda width: pl.BlockSpec((tm, width), lambda i: (i, 0))
    block_bytes = (2 * _nbytes((tm, d), F32) + _nbytes((tm, wa + wb), BF16) + _nbytes(w.shape[1:], BF16))
    return pl.pallas_call(
        _out_proj_kernel,
        grid=(n // tm,),
        in_specs=[row(d), row(wa), row(wb), pl.BlockSpec((None,) + w.shape[1:], lambda i: (layer, 0, 0))],
        out_specs=row(d),
        out_shape=jax.ShapeDtypeStruct((n, d), F32),
        compiler_params=pltpu.CompilerParams(
            dimension_semantics=("arbitrary",), vmem_limit_bytes=_vmem_limit(block_bytes)),
        name="out_proj",
    )(x, oa, ob, w)


def _mlp_kernel(x_ref, g_ref, wu_ref, wd_ref, gn_ref, o_ref, *scratch_or_out, last_layer):
    if last_layer:
        (xn_ref,) = scratch_or_out
    else:
        xn_out_ref, xn_ref = scratch_or_out
    f = pl.program_id(1)

    @pl.when(f == 0)
    def _():
        x = x_ref[...]
        xn_ref[...] = _rmsnorm(x, g_ref[...]).astype(BF16)
        o_ref[...] = x

    h = jnp.dot(xn_ref[...], wu_ref[...], preferred_element_type=F32)
    h = jnp.square(jnp.maximum(h, 0.0)).astype(BF16)
    o_ref[...] += jnp.dot(h, wd_ref[...], preferred_element_type=F32)

    @pl.when(f == pl.num_programs(1) - 1)
    def _():
        y = _rmsnorm(o_ref[...], gn_ref[...])
        if last_layer:
            o_ref[...] = y
        else:
            xn_out_ref[...] = y.astype(BF16)


def _mlp(x, g, wu, wd, g_next, layer, next_row, *, tm, last_layer):
    n, d = x.shape
    _, n_f, _, tf = wu.shape
    row = pl.BlockSpec((tm, d), lambda i, f: (i, 0))
    x_spec = pl.BlockSpec((tm, d), lambda i, f: (i, 0), pipeline_mode=pl.Buffered(1))
    block_bytes = (2 * _nbytes((tm, d), F32) + 2 * _nbytes((d, tf), BF16) + 2 * _nbytes((tm, d), BF16)
                   + _nbytes((tm, tf), F32))
    f32_out = jax.ShapeDtypeStruct((n, d), F32)
    return pl.pallas_call(
        functools.partial(_mlp_kernel, last_layer=last_layer),
        grid=(n // tm, n_f),
        in_specs=[x_spec, _layer_vec(d, layer),
                  pl.BlockSpec((None, None, d, tf), lambda i, f: (layer, f, 0, 0)),
                  pl.BlockSpec((None, tf, d), lambda i, f: (layer, f, 0)),
                  _layer_vec(d, next_row)],
        out_specs=row if last_layer else [row, row],
        out_shape=f32_out if last_layer else [f32_out, jax.ShapeDtypeStruct((n, d), BF16)],
        scratch_shapes=[pltpu.VMEM((tm, d), BF16)],
        compiler_params=pltpu.CompilerParams(
            dimension_semantics=("arbitrary", "arbitrary"), vmem_limit_bytes=_vmem_limit(block_bytes)),
        name="mlp",
    )(x, g, wu, wd, g_next)


def _head_bias(q_rows, t_pad, cols, n_heads):
    r = lax.broadcasted_iota(jnp.int32, (q_rows, cols), 0)
    c = lax.broadcasted_iota(jnp.int32, (q_rows, cols), 1)
    return jnp.where(c % n_heads == (r // t_pad) % n_heads, 0.0, NEG)


def _new_key_ok(q_rows, cols, n_heads, n_new, t_pad):
    tok_q = lax.broadcasted_iota(jnp.int32, (q_rows, cols), 0) % t_pad
    tok_k = lax.broadcasted_iota(jnp.int32, (q_rows, cols), 1) // n_heads
    return (tok_k <= tok_q) & (tok_k < n_new)


def _page_attn_step(q, kn_ref, vn_ref, k_refs, v_refs, bias_ref, state_refs, *, t_pad, n_heads, n_new, penalty):
    p = pl.program_id(1)
    q_rows = q.shape[0]
    page_rows = kn_ref.shape[0]

    @pl.when(p == 0)
    def _():
        bias = _head_bias(q_rows, t_pad, page_rows, n_heads)
        bias_ref[...] = bias
        s = _scores(q, kn_ref[...].astype(BF16)) + bias
        s = jnp.where(_new_key_ok(q_rows, page_rows, n_heads, n_new, t_pad), s, NEG)
        for r, val in zip(state_refs, _softmax_first(s, vn_ref[...].astype(BF16))):
            r[...] = val

    bias = bias_ref[...]
    parts = []
    for i, k in enumerate(k_refs):
        part = _scores(q, k[...].astype(BF16)) + bias
        parts.append(part if penalty is None else part + penalty(i))
    s = jnp.concatenate(parts, axis=1)
    v = jnp.concatenate([v[...].astype(BF16) for v in v_refs], axis=0)
    state = _softmax_update(tuple(r[...] for r in state_refs), s, v)
    for r, val in zip(state_refs, state):
        r[...] = val


def _dec_diff_kernel(pt_ref, lq1, lk1, lq2, lk2, g_ref, q_ref, kn_ref, vn_ref, *refs,
                     lam_init, n_new, n_heads, t_pad, pps):
    del pt_ref
    k_refs, v_refs, o_ref = refs[:pps], refs[pps:2 * pps], refs[2 * pps]
    bias_ref, m_ref, l_ref, acc_ref = refs[2 * pps + 1:]
    _page_attn_step(q_ref[...].astype(BF16), kn_ref, vn_ref, k_refs, v_refs, bias_ref, (m_ref, l_ref, acc_ref),
                    t_pad=t_pad, n_heads=n_heads, n_new=n_new, penalty=None)

    @pl.when(pl.program_id(1) == pl.num_programs(1) - 1)
    def _():
        lam = _lambda(lq1, lk1, lq2, lk2, lam_init)
        o = acc_ref[...] / l_ref[...]
        half = o.shape[0] // 2
        o_ref[...] = _subln(o[:half] - lam * o[half:], g_ref[...], lam_init)


def _paged_specs(page_table, cache, layer, pps):
    page_rows = cache.shape[2]

    def spec(i):
        return pl.BlockSpec((None, None, page_rows, HEAD_DIM),
                            lambda bi, p, pt: (layer, pt[bi, p * pps + i], 0, 0))
    return [spec(i) for i in range(pps)]


def _dec_diff_attn(page_table, q, k_new, v_new, cache_k, cache_v, lams, g, *,
                   layer, lam_init, n_new, n_heads, t_pad, pps):
    b, q_rows, _ = q.shape
    n_pages = page_table.shape[1]
    page_rows = cache_k.shape[2]
    vec = lambda width: pl.BlockSpec((None, 1, width), lambda bi, p, pt: (layer, 0, 0))
    per_b = lambda r: pl.BlockSpec((None, r, HEAD_DIM), lambda bi, p, pt: (bi, 0, 0))
    block_bytes = (2 + 2 * pps) * _nbytes((page_rows, HEAD_DIM), F32) + 4 * pps * _nbytes((q_rows, page_rows), F32)
    return pl.pallas_call(
        functools.partial(_dec_diff_kernel, lam_init=lam_init, n_new=n_new, n_heads=n_heads, t_pad=t_pad, pps=pps),
        grid_spec=pltpu.PrefetchScalarGridSpec(
            num_scalar_prefetch=1,
            grid=(b, n_pages // pps),
            in_specs=[vec(DQK_A)] * 4 + [vec(HEAD_DIM), per_b(q_rows), per_b(page_rows), per_b(page_rows)]
                     + _paged_specs(page_table, cache_k, layer, pps) + _paged_specs(page_table, cache_v, layer, pps),
            out_specs=per_b(q_rows // 2),
            scratch_shapes=[pltpu.VMEM((q_rows, page_rows), F32), pltpu.VMEM((q_rows, 1), F32),
                            pltpu.VMEM((q_rows, 1), F32), pltpu.VMEM((q_rows, HEAD_DIM), F32)]),
        out_shape=jax.ShapeDtypeStruct((b, q_rows // 2, HEAD_DIM), F32),
        compiler_params=pltpu.CompilerParams(
            dimension_semantics=("arbitrary", "arbitrary"), vmem_limit_bytes=_vmem_limit(block_bytes)),
        name="dec_diff_attn",
    )(page_table, *lams, g, q, k_new, v_new, *([cache_k] * pps), *([cache_v] * pps))


def _dec_kmean_kernel(pt_ref, *refs, n_heads, ppb):
    del pt_ref
    o_ref = refs[-1]
    sums = [jnp.sum(r[...].reshape(-1, n_heads, HEAD_DIM), axis=0) for r in refs[:-1]]
    for i in range(len(sums) // ppb):
        o_ref[i * n_heads:(i + 1) * n_heads, :] = sum(sums[i * ppb:(i + 1) * ppb]) * (1.0 / MOBA_BLOCK)


def _dec_kmean(page_table, cache_k, *, layer, n_heads, pps):
    b, n_pages = page_table.shape
    page_rows = cache_k.shape[2]
    ppb = MOBA_BLOCK // (page_rows // n_heads)
    nblk = n_pages // ppb
    bps = pps // ppb
    return pl.pallas_call(
        functools.partial(_dec_kmean_kernel, n_heads=n_heads, ppb=ppb),
        grid_spec=pltpu.PrefetchScalarGridSpec(
            num_scalar_prefetch=1,
            grid=(b, n_pages // pps),
            in_specs=_paged_specs(page_table, cache_k, layer, pps),
            out_specs=pl.BlockSpec((None, bps * n_heads, HEAD_DIM), lambda bi, n, pt: (bi, n, 0))),
        out_shape=jax.ShapeDtypeStruct((b, nblk * n_heads, HEAD_DIM), F32),
        compiler_params=pltpu.CompilerParams(
            dimension_semantics=("arbitrary", "arbitrary"),
            vmem_limit_bytes=_vmem_limit(pps * _nbytes((page_rows, HEAD_DIM), F32))),
        name="dec_kmean",
    )(page_table, *([cache_k] * pps))


def _dec_select_kernel(q_ref, kmean_ref, o_ref, *, n_heads, t_pad):
    gate = lax.dot_general(q_ref[...], kmean_ref[...], _NT, precision=lax.Precision.HIGHEST,
                           preferred_element_type=F32)
    r = lax.broadcasted_iota(jnp.int32, gate.shape, 0)
    c = lax.broadcasted_iota(jnp.int32, gate.shape, 1)
    gate = jnp.where(c % n_heads == r // t_pad, gate, -jnp.inf)
    col = c.astype(F32)
    chosen = jnp.zeros(gate.shape, F32)
    for _ in range(MOBA_TOPK):
        best = jnp.max(gate, axis=-1, keepdims=True)
        pick = jnp.min(jnp.where(gate == best, col, float(gate.shape[1])), axis=-1, keepdims=True)
        chosen = jnp.where(col == pick, 1.0, chosen)
        gate = jnp.where(col == pick, -jnp.inf, gate)
    o_ref[...] = chosen


def _dec_select(q, kmean, *, n_heads, t_pad):
    b, q_rows, _ = q.shape
    km_rows = kmean.shape[1]
    return pl.pallas_call(
        functools.partial(_dec_select_kernel, n_heads=n_heads, t_pad=t_pad),
        grid=(b,),
        in_specs=[pl.BlockSpec((None, q_rows, HEAD_DIM), lambda bi: (bi, 0, 0)),
                  pl.BlockSpec((None, km_rows, HEAD_DIM), lambda bi: (bi, 0, 0))],
        out_specs=pl.BlockSpec((None, q_rows, km_rows), lambda bi: (bi, 0, 0)),
        out_shape=jax.ShapeDtypeStruct((b, q_rows, km_rows), F32),
        compiler_params=pltpu.CompilerParams(dimension_semantics=("arbitrary",)),
        name="dec_select",
    )(q, kmean)


def _dec_moba_kernel(pt_ref, q_ref, sel_ref, kn_ref, vn_ref, *refs, n_new, n_heads, t_pad, pps, ppb):
    del pt_ref
    k_refs, v_refs, o_ref = refs[:pps], refs[pps:2 * pps], refs[2 * pps]
    bias_ref, m_ref, l_ref, acc_ref = refs[2 * pps + 1:]
    sel = sel_ref[...]
    blk_lane = lax.broadcasted_iota(jnp.int32, sel.shape, 1) // n_heads
    first_blk = pl.program_id(1) * (pps // ppb)
    penalties = []
    for i in range(pps // ppb):
        chosen = jnp.sum(jnp.where(blk_lane == first_blk + i, sel, 0.0), axis=-1, keepdims=True)
        penalties.append((1.0 - chosen) * NEG)
    q = (q_ref[...] * HEAD_DIM ** -0.5).astype(BF16)
    _page_attn_step(q, kn_ref, vn_ref, k_refs, v_refs, bias_ref, (m_ref, l_ref, acc_ref),
                    t_pad=t_pad, n_heads=n_heads, n_new=n_new, penalty=lambda i: penalties[i // ppb])

    @pl.when(pl.program_id(1) == pl.num_programs(1) - 1)
    def _():
        o_ref[...] = acc_ref[...] / l_ref[...]


def _dec_moba_attn(page_table, q, sel, k_new, v_new, cache_k, cache_v, *, layer, n_new, n_heads, t_pad, pps):
    b, q_rows, _ = q.shape
    n_pages = page_table.shape[1]
    page_rows = cache_k.shape[2]
    ppb = MOBA_BLOCK // (page_rows // n_heads)
    per_b = lambda r, c: pl.BlockSpec((None, r, c), lambda bi, p, pt: (bi, 0, 0))
    block_bytes = (2 + 2 * pps) * _nbytes((page_rows, HEAD_DIM), F32) + 4 * pps * _nbytes((q_rows, page_rows), F32)
    return pl.pallas_call(
        functools.partial(_dec_moba_kernel, n_new=n_new, n_heads=n_heads, t_pad=t_pad, pps=pps, ppb=ppb),
        grid_spec=pltpu.PrefetchScalarGridSpec(
            num_scalar_prefetch=1,
            grid=(b, n_pages // pps),
            in_specs=[per_b(q_rows, HEAD_DIM), per_b(q_rows, sel.shape[2]),
                      per_b(page_rows, HEAD_DIM), per_b(page_rows, HEAD_DIM)]
                     + _paged_specs(page_table, cache_k, layer, pps) + _paged_specs(page_table, cache_v, layer, pps),
            out_specs=per_b(q_rows, HEAD_DIM),
            scratch_shapes=[pltpu.VMEM((q_rows, page_rows), F32), pltpu.VMEM((q_rows, 1), F32),
                            pltpu.VMEM((q_rows, 1), F32), pltpu.VMEM((q_rows, HEAD_DIM), F32)]),
        out_shape=jax.ShapeDtypeStruct((b, q_rows, HEAD_DIM), F32),
        compiler_params=pltpu.CompilerParams(
            dimension_semantics=("arbitrary", "arbitrary"), vmem_limit_bytes=_vmem_limit(block_bytes)),
        name="dec_moba_attn",
    )(page_table, q, sel, k_new, v_new, *([cache_k] * pps), *([cache_v] * pps))


def _lam_init(layer):
    return 0.8 - 0.6 * math.exp(-0.3 * layer)


def _stack_queries(q, b, t, n_heads, t_pad):
    q = q.astype(F32).reshape(b, t, n_heads, HEAD_DIM).transpose(0, 2, 1, 3)
    return jnp.pad(q, ((0, 0), (0, 0), (0, t_pad - t), (0, 0)))


def _unstack_heads(o, b, t, n_heads, t_pad):
    o = o.reshape(b, n_heads, t_pad, HEAD_DIM)[:, :, :t].transpose(0, 2, 1, 3)
    return o.reshape(b * t, n_heads * HEAD_DIM).astype(BF16)


def _new_page(tall, b, t, n_heads, page):
    a = tall.reshape(b, t * n_heads, HEAD_DIM)
    return jnp.pad(a, ((0, 0), (0, (page - t) * n_heads), (0, 0)))


def kernel(x_prompt, x_sample, cache_a_k, cache_a_v, cache_b_k, cache_b_v, page_table, norm_attn, w_in, lambda_q1, lambda_k1, lambda_q2, lambda_k2, subln_a, w_out, norm_mlp, w_up, w_down, norm_final):
    depth, d, d_in = w_in.shape
    dff = w_up.shape[2]
    bp, tp, _ = x_prompt.shape
    bs, ts, _ = x_sample.shape
    n_pool, page = cache_a_k.shape[1], cache_a_k.shape[2]
    n_pages = page_table.shape[1]
    past_len = n_pages * page
    n_heads = cache_a_k.shape[3]
    sec = n_heads * HEAD_DIM
    ppb = MOBA_BLOCK // page
    assert cache_b_k.shape[3] == n_heads and d_in == N_SECTIONS * sec
    assert ts <= 8 and past_len % MOBA_BLOCK == 0 and past_len // MOBA_BLOCK >= MOBA_TOPK
    t_pad = 4 if ts <= 4 else 8
    pps = DEC_PAGES_PER_STEP
    assert MOBA_BLOCK % page == 0 and n_pages % pps == 0 and pps % ppb == 0
    assert n_pages % KMEAN_PAGES_PER_STEP == 0 and KMEAN_PAGES_PER_STEP % ppb == 0 and (n_heads * t_pad) % 8 == 0
    tm_p = min(TM_DENSE, bp * tp)
    tm_proj = TM_PROJ if tp % TM_PROJ == 0 else tm_p
    tm_mlp = TM_MLP if (bp * tp) % TM_MLP == 0 else tm_p
    tq = min(TQ_ATTN, tp)
    tf = min(TF_MLP, dff)
    assert tp % tq == 0 and tq % MOBA_BLOCK == 0 and (bp * tp) % tm_p == 0 and tp % tm_p == 0 and dff % tf == 0

    w_in_b = _tile_columns_bf16(w_in, sec)
    w_up_b = _tile_columns_bf16(w_up, tf)
    w_out_b, w_down_b = w_out.astype(BF16), w_down.astype(BF16)
    caches = tuple(c.reshape(depth, n_pool, page * n_heads, HEAD_DIM)
                   for c in (cache_a_k, cache_a_v, cache_b_k, cache_b_v))
    norm_attn3, norm_mlp3, subln3 = norm_attn[:, None], norm_mlp[:, None], subln_a[:, None]
    norm_final3 = norm_final[None, None]
    lams = tuple(v[:, None] for v in (lambda_q1, lambda_k1, lambda_q2, lambda_k2))

    pos_p = jnp.arange(tp)
    pos_s = jnp.tile(past_len + jnp.arange(ts), bs)
    tabs_p = (_rope_tables(pos_p, ROT_A, DQK_A), _rope_tables(pos_p, ROT_B, HEAD_DIM))
    tabs_s = (_rope_tables(pos_s, ROT_A, DQK_A), _rope_tables(pos_s, ROT_B, HEAD_DIM))
    comp_mask = (jnp.arange(HEAD_DIM)[None, :] // DQK_A == jnp.arange(2)[:, None]).astype(F32)

    tm_s = bs * ts

    def dense_tail(x, oa, ob, l, tm, tm_mlp):
        x = _out_proj(x, oa, ob, w_out_b, l, tm=tm)
        last = l == depth - 1
        g_next, next_row = (norm_final3, 0) if last else (norm_attn3, l + 1)
        return _mlp(x, norm_mlp3, w_up_b, w_down_b, g_next, l, next_row, tm=tm_mlp, last_layer=last)

    xp = x_prompt.reshape(bp * tp, d)
    xs = x_sample.reshape(bs * ts, d)
    xnp = _norm(xp, norm_attn3, 0, tm=tm_p)
    xns = _norm(xs, norm_attn3, 0, tm=tm_s)
    tall_p, tall_s = (), ()
    for l in range(depth):
        lam_init = _lam_init(l)

        qa, ka_w, ka_t, va_w, va_t, qb, kb_w, kb_t, vb_w, vb_t = _proj(xnp, w_in_b, l, *tabs_p, tall_p, tm=tm_proj)
        tall_p = (ka_t, va_t, kb_t, vb_t)
        as3 = lambda a: a.reshape(bp, tp, sec)
        oa = _diff_attn(as3(qa), as3(ka_w), as3(va_w), lams, subln3, l, tq=tq, lam_init=lam_init)
        ob = _moba_attn(as3(qb), as3(kb_w), as3(vb_w), tq=tq)
        out = dense_tail(xp, oa.reshape(bp * tp, sec), ob.reshape(bp * tp, sec), l, tm_p, tm_mlp)
        xp, xnp = (out, None) if l == depth - 1 else out

        qa, _, ka_t, _, va_t, qb, _, kb_t, _, vb_t = _proj(xns, w_in_b, l, *tabs_s, tall_s, tm=tm_s)
        tall_s = (ka_t, va_t, kb_t, vb_t)
        qa_st = _stack_queries(qa, bs, ts, n_heads, t_pad)[:, None] * comp_mask[None, :, None, None, :]
        qa_st = qa_st.reshape(bs, 2 * n_heads * t_pad, HEAD_DIM)
        qb_st = _stack_queries(qb, bs, ts, n_heads, t_pad).reshape(bs, n_heads * t_pad, HEAD_DIM)
        new_pages = tuple(_new_page(a[l], bs, ts, n_heads, page) for a in tall_s)
        oa = _dec_diff_attn(page_table, qa_st, new_pages[0], new_pages[1], caches[0], caches[1], lams, subln3,
                            layer=l, lam_init=lam_init, n_new=ts, n_heads=n_heads, t_pad=t_pad, pps=pps)
        kmean = _dec_kmean(page_table, caches[2], layer=l, n_heads=n_heads, pps=KMEAN_PAGES_PER_STEP)
        sel = _dec_select(qb_st, kmean, n_heads=n_heads, t_pad=t_pad)
        ob = _dec_moba_attn(page_table, qb_st, sel, new_pages[2], new_pages[3], caches[2], caches[3],
                            layer=l, n_new=ts, n_heads=n_heads, t_pad=t_pad, pps=pps)
        out = dense_tail(xs, _unstack_heads(oa, bs, ts, n_heads, t_pad), _unstack_heads(ob, bs, ts, n_heads, t_pad),
                         l, tm_s, tm_s)
        xs, xns = (out, None) if l == depth - 1 else out

    new_p = tuple(a.reshape(depth, bp, tp, n_heads, HEAD_DIM) for a in tall_p)
    new_s = tuple(a.reshape(depth, bs, ts, n_heads, HEAD_DIM) for a in tall_s)
    return (xp.reshape(bp, tp, d), xs.reshape(bs, ts, d)) + new_p + new_s
```
